```python
import math
import jax, jax.numpy as jnp
from jax import lax
import numpy as np

D_MODEL = 1024
BATCH = 16
SEQ = 4096
DEPTH = 1

LRU_WIDTH = D_MODEL
LRU_HEADS = 16
LRU_HEAD_DIM = LRU_WIDTH // LRU_HEADS
LRU_C = 8.0
CONV_WIDTH = 4
SSD_INNER = 2 * D_MODEL
SSD_HEAD_DIM = 64
SSD_HEADS = SSD_INNER // SSD_HEAD_DIM
SSD_GROUPS = 8
SSD_HPG = SSD_HEADS // SSD_GROUPS
SSD_STATE = 128
SSD_CHUNK = 128
SSD_CONV_DIM = SSD_INNER + 2 * SSD_GROUPS * SSD_STATE
N_BRANCH = 2
D_FF = 4 * D_MODEL
N_MOD = 6
EPS = 1e-6
IN_SIZES = (LRU_WIDTH, LRU_WIDTH, SSD_INNER, SSD_CONV_DIM, SSD_HEADS, N_BRANCH * D_MODEL)
IN_DIM = sum(IN_SIZES)

kernel_name = "hybrid_rglru_ssd_gated_merge_block"


def rmsnorm(x, w):
    xf = x.astype(jnp.float32)
    y = xf * lax.rsqrt(jnp.mean(xf * xf, axis=-1, keepdims=True) + EPS)
    return (y * w.astype(jnp.float32)).astype(x.dtype)


def grouped_rmsnorm(y, w):
    b, s, d = y.shape
    yf = y.astype(jnp.float32).reshape(b, s, SSD_GROUPS, d // SSD_GROUPS)
    yf = yf * lax.rsqrt(jnp.mean(yf * yf, axis=-1, keepdims=True) + EPS)
    return (yf.reshape(b, s, d) * w.astype(jnp.float32)).astype(y.dtype)


def causal_dwconv(x, w, b):
    s = x.shape[1]
    xp = jnp.pad(x, ((0, 0), (CONV_WIDTH - 1, 0), (0, 0)))
    return sum((xp[:, k:k + s] * w[k] for k in range(CONV_WIDTH)), b)


def rglru(x, w_a, b_a, w_x, b_x, lam):
    bsz, s, _ = x.shape
    xh = x.reshape(bsz, s, LRU_HEADS, LRU_HEAD_DIM)
    r = jax.nn.sigmoid(jnp.einsum('bshi,hij->bshj', xh, w_a).reshape(bsz, s, LRU_WIDTH) + b_a)
    i = jax.nn.sigmoid(jnp.einsum('bshi,hij->bshj', xh, w_x).reshape(bsz, s, LRU_WIDTH) + b_x)
    log_a = (-LRU_C * r.astype(jnp.float32)) * jax.nn.softplus(-lam.astype(jnp.float32))
    a = jnp.exp(log_a)
    u = jnp.sqrt(-jnp.expm1(2.0 * log_a)) * (i * x).astype(jnp.float32)

    def combine(left, right):
        a1, b1 = left
        a2, b2 = right
        return a1 * a2, a2 * b1 + b2

    _, h = lax.associative_scan(combine, (a, u), axis=1)
    return h.astype(x.dtype)


def ssd_chunked_scan(xs, dt, A, Bm, Cm):
    bsz, s = xs.shape[:2]
    nc, L = s // SSD_CHUNK, SSD_CHUNK

    def to_chunks(t):
        return jnp.moveaxis(t.reshape((bsz, nc, L) + t.shape[2:]), 1, 0)

    xdt = (xs * dt[..., None]).reshape(bsz, s, SSD_GROUPS, SSD_HPG, SSD_HEAD_DIM)
    dA = (dt * A).reshape(bsz, s, SSD_GROUPS, SSD_HPG)
    causal = jnp.tril(jnp.ones((L, L), dtype=bool))[None, :, :, None, None]

    def step(state, inp):
        x_c, dA_c, B_c, C_c = inp
        cs = jnp.cumsum(dA_c, axis=1)
        seg = cs[:, :, None] - cs[:, None, :]
        decay = jnp.exp(jnp.where(causal, seg, -jnp.inf))
        cb = jnp.einsum('blgn,bsgn->blsg', C_c, B_c)
        y_diag = jnp.einsum('blsge,bsgep->blgep', cb[..., None] * decay, x_c)
        y_off = jnp.einsum('blgn,bgepn->blgep', C_c, state) * jnp.exp(cs)[..., None]
        decay_to_end = jnp.exp(cs[:, -1:] - cs)
        new_state = state * jnp.exp(cs[:, -1])[..., None, None] + jnp.einsum(
            'blgn,blgep->bgepn', B_c, x_c * decay_to_end[..., None])
        return new_state, y_diag + y_off

    init = jnp.zeros((bsz, SSD_GROUPS, SSD_HPG, SSD_HEAD_DIM, SSD_STATE), jnp.float32)
    _, y = lax.scan(step, init, (to_chunks(xdt), to_chunks(dA), to_chunks(Bm), to_chunks(Cm)))
    return jnp.moveaxis(y, 0, 1).reshape(bsz, s, SSD_HEADS, SSD_HEAD_DIM)


def ssd_branch(z, xbc, dt_raw, conv_w, conv_b, dt_bias, a_log, d_skip, norm_w):
    bsz, s, _ = z.shape
    xbc = jax.nn.silu(causal_dwconv(xbc, conv_w, conv_b))
    xs, Bm, Cm = jnp.split(xbc, [SSD_INNER, SSD_INNER + SSD_GROUPS * SSD_STATE], axis=-1)
    xs = xs.astype(jnp.float32).reshape(bsz, s, SSD_HEADS, SSD_HEAD_DIM)
    Bm = Bm.astype(jnp.float32).reshape(bsz, s, SSD_GROUPS, SSD_STATE)
    Cm = Cm.astype(jnp.float32).reshape(bsz, s, SSD_GROUPS, SSD_STATE)
    dt = jax.nn.softplus(dt_raw.astype(jnp.float32) + dt_bias.astype(jnp.float32))
    A = -jnp.exp(a_log.astype(jnp.float32))
    y = ssd_chunked_scan(xs, dt, A, Bm, Cm)
    y = y + d_skip.astype(jnp.float32)[:, None] * xs
    y = y.reshape(bsz, s, SSD_INNER) * jax.nn.silu(z.astype(jnp.float32))
    return grouped_rmsnorm(y, norm_w).astype(z.dtype)


def temporal_mixer(h, w_in, b_gate, lru_conv_w, lru_conv_b, lru_wa, lru_ba, lru_wx, lru_bx,
                   lru_lambda, w_pa, ssd_conv_w, ssd_conv_b, ssd_dt_bias, ssd_a_log, ssd_d,
                   ssd_norm_w, w_pb, w_out):
    bsz, s, _ = h.shape
    proj = h @ w_in
    lru_x, lru_g, ssd_z, ssd_xbc, ssd_dt, gates = jnp.split(
        proj, [int(v) for v in np.cumsum(IN_SIZES)[:-1]], axis=-1)
    ra = rglru(causal_dwconv(lru_x, lru_conv_w, lru_conv_b), lru_wa, lru_ba, lru_wx, lru_bx, lru_lambda)
    y_a = (ra * jax.nn.gelu(lru_g)) @ w_pa
    y_b = ssd_branch(ssd_z, ssd_xbc, ssd_dt, ssd_conv_w, ssd_conv_b, ssd_dt_bias, ssd_a_log,
                     ssd_d, ssd_norm_w) @ w_pb
    g = jax.nn.sigmoid(gates + b_gate).reshape(bsz, s, N_BRANCH, D_MODEL)
    merged = g[:, :, 0] * y_a + g[:, :, 1] * y_b
    return merged @ w_out


def setup_inputs(seed: int = 0) -> dict:
    key = jax.random.key(seed)
    ks = jax.random.split(key, 32)
    f32 = jnp.float32

    def nrm(k, shape, scale):
        return jax.random.normal(k, shape, f32) * scale

    def gain(k, shape):
        return 1.0 + 0.1 * jax.random.normal(k, shape, f32)

    L = DEPTH
    u = jax.random.uniform(ks[12], (L, LRU_WIDTH), f32, minval=0.9, maxval=0.999)
    sroot = u ** (1.0 / LRU_C)
    lru_lambda = jnp.log(sroot) - jnp.log1p(-sroot)
    dt0 = jnp.exp(jax.random.uniform(ks[16], (L, SSD_HEADS), f32,
                                     minval=math.log(1e-3), maxval=math.log(1e-1)))
    ssd_dt_bias = dt0 + jnp.log(-jnp.expm1(-dt0))
    ssd_a_log = jnp.log(jax.random.uniform(ks[17], (L, SSD_HEADS), f32, minval=1.0, maxval=16.0))
    return {
        "x": jax.random.normal(ks[0], (BATCH, SEQ, D_MODEL), f32),
        "c": jax.random.normal(ks[1], (BATCH, D_MODEL), f32),
        "w_ada": nrm(ks[2], (L, D_MODEL, N_MOD * D_MODEL), 0.5 * D_MODEL ** -0.5),
        "b_ada": nrm(ks[3], (L, N_MOD * D_MODEL), 0.01),
        "pre_norm1": gain(ks[4], (L, D_MODEL)),
        "post_norm1": gain(ks[5], (L, D_MODEL)),
        "w_in": nrm(ks[6], (L, D_MODEL, IN_DIM), D_MODEL ** -0.5),
        "b_gate": nrm(ks[7], (L, N_BRANCH * D_MODEL), 0.01),
        "lru_conv_w": nrm(ks[8], (L, CONV_WIDTH, LRU_WIDTH), CONV_WIDTH ** -0.5),
        "lru_conv_b": nrm(ks[9], (L, LRU_WIDTH), 0.01),
        "lru_wa": nrm(ks[10], (L, LRU_HEADS, LRU_HEAD_DIM, LRU_HEAD_DIM), LRU_HEAD_DIM ** -0.5),
        "lru_ba": nrm(ks[11], (L, LRU_WIDTH), 0.01),
        "lru_wx": nrm(ks[13], (L, LRU_HEADS, LRU_HEAD_DIM, LRU_HEAD_DIM), LRU_HEAD_DIM ** -0.5),
        "lru_bx": nrm(ks[14], (L, LRU_WIDTH), 0.01),
        "lru_lambda": lru_lambda,
        "w_pa": nrm(ks[15], (L, LRU_WIDTH, D_MODEL), LRU_WIDTH ** -0.5),
        "ssd_conv_w": nrm(ks[18], (L, CONV_WIDTH, SSD_CONV_DIM), CONV_WIDTH ** -0.5),
        "ssd_conv_b": nrm(ks[19], (L, SSD_CONV_DIM), 0.01),
        "ssd_dt_bias": ssd_dt_bias,
        "ssd_a_log": ssd_a_log,
        "ssd_d": gain(ks[20], (L, SSD_HEADS)),
        "ssd_norm_w": gain(ks[21], (L, SSD_INNER)),
        "w_pb": nrm(ks[22], (L, SSD_INNER, D_MODEL), SSD_INNER ** -0.5),
        "w_out": nrm(ks[23], (L, D_MODEL, D_MODEL), D_MODEL ** -0.5),
        "pre_norm2": gain(ks[24], (L, D_MODEL)),
        "post_norm2": gain(ks[25], (L, D_MODEL)),
        "w_ff1": nrm(ks[26], (L, D_MODEL, D_FF), D_MODEL ** -0.5),
        "w_ff2": nrm(ks[27], (L, D_FF, D_MODEL), D_FF ** -0.5),
    }


def reference(x, c, w_ada, b_ada, pre_norm1, post_norm1, w_in, b_gate, lru_conv_w, lru_conv_b,
              lru_wa, lru_ba, lru_wx, lru_bx, lru_lambda, w_pa, ssd_conv_w, ssd_conv_b,
              ssd_dt_bias, ssd_a_log, ssd_d, ssd_norm_w, w_pb, w_out, pre_norm2, post_norm2,
              w_ff1, w_ff2):
    c_act = jax.nn.silu(c)
    for l in range(DEPTH):
        mod = (c_act @ w_ada[l] + b_ada[l])[:, None, :]
        sh1, sc1, g1, sh2, sc2, g2 = jnp.split(mod, N_MOD, axis=-1)
        h = rmsnorm(x, pre_norm1[l]) * (1.0 + sc1) + sh1
        y = temporal_mixer(h, w_in[l], b_gate[l], lru_conv_w[l], lru_conv_b[l], lru_wa[l],
                           lru_ba[l], lru_wx[l], lru_bx[l], lru_lambda[l], w_pa[l],
                           ssd_conv_w[l], ssd_conv_b[l], ssd_dt_bias[l], ssd_a_log[l], ssd_d[l],
                           ssd_norm_w[l], w_pb[l], w_out[l])
        x = x + g1 * rmsnorm(y, post_norm1[l])
        h = rmsnorm(x, pre_norm2[l]) * (1.0 + sc2) + sh2
        y = jnp.square(jax.nn.relu(h @ w_ff1[l])) @ w_ff2[l]
        x = x + g2 * rmsnorm(y, post_norm2[l])
    return x
```

```python
import functools
import math

import jax
import jax.numpy as jnp
from jax import lax
from jax.experimental import pallas as pl
from jax.experimental.pallas import tpu as pltpu

F32 = jnp.float32
BF16 = jnp.bfloat16

D = 1024
LRU_W = D
LRU_HEADS = 16
LRU_HD = LRU_W // LRU_HEADS
LRU_C = 8.0
CONV_K = 4
SSD_INNER = 2 * D
SSD_P = 64
SSD_H = SSD_INNER // SSD_P
SSD_G = 8
SSD_HPG = SSD_H // SSD_G
SSD_N = 128
SSD_L = 128
SSD_GW = SSD_HPG * SSD_P
SSD_BC = SSD_G * SSD_N
SSD_CONV = SSD_INNER + 2 * SSD_BC
D_FF = 4 * D
N_MOD = 6
EPS = 1e-6

LANES = 128
SUBLANES = 8
HP = LANES
OFF_LX, OFF_LG, OFF_Z, OFF_XBC = 0, LRU_W, 2 * LRU_W, 2 * LRU_W + SSD_INNER
OFF_GT = OFF_XBC + SSD_CONV
OFF_DT = OFF_GT + 2 * D
W_IN_P = OFF_DT + HP

VMEM_LIMIT = 56 * 1024 * 1024


def _sigmoid(v):
    return 0.5 * jnp.tanh(0.5 * v) + 0.5


def _softplus(v):
    return jnp.maximum(v, 0.0) + jnp.log1p(jnp.exp(-jnp.abs(v)))


def _gelu_tanh(v):
    c = math.sqrt(2.0 / math.pi)
    return 0.5 * v * (1.0 + jnp.tanh(c * (v + 0.044715 * (v * v * v))))


def _rms_scale(v):
    return lax.rsqrt(jnp.mean(v * v, axis=-1, keepdims=True) + EPS)


def _split_bf16(v, n):
    parts = []
    r = v
    for _ in range(n):
        p = r.astype(BF16)
        parts.append(p)
        r = r - p.astype(F32)
    return parts


def _dot(a, b):
    return jnp.dot(a, b, preferred_element_type=F32)


def _adaln_kernel(c_ref, w_ref, b_ref, o_ref):
    c = c_ref[...]
    ca = (c * _sigmoid(c)).astype(BF16)
    o_ref[...] = _dot(ca, w_ref[...].astype(BF16)) + b_ref[...]


def _adaln(c, w_ada, b_ada):
    bsz = c.shape[0]
    n = w_ada.shape[1]
    tn = D
    return pl.pallas_call(
        _adaln_kernel,
        grid=(n // tn,),
        in_specs=[
            pl.BlockSpec((bsz, D), lambda j: (0, 0)),
            pl.BlockSpec((D, tn), lambda j: (0, j)),
            pl.BlockSpec((1, tn), lambda j: (0, j)),
        ],
        out_specs=pl.BlockSpec((bsz, tn), lambda j: (0, j)),
        out_shape=jax.ShapeDtypeStruct((bsz, n), F32),
        name="adaln_mod",
    )(c, w_ada, b_ada.reshape(1, n))


def _inproj_kernel(x_ref, sc_ref, sh_ref, nw_ref, w_ref, o_ref, *, col_chunk):
    x = x_ref[0]
    h = (x * _rms_scale(x)) * nw_ref[...]
    h = h * (1.0 + sc_ref[0]) + sh_ref[0]
    hb = h.astype(BF16)
    for j in range(W_IN_P // col_chunk):
        sl = slice(j * col_chunk, (j + 1) * col_chunk)
        o_ref[0, :, sl] = _dot(hb, w_ref[:, sl]).astype(BF16)


def _const_spec(shape):
    nd = len(shape)
    return pl.BlockSpec(shape, lambda b, i: (0,) * nd, pipeline_mode=pl.Buffered(1))


def _in_proj(x, sc, sh, nw, w_in_p, tm):
    bsz, seq, _ = x.shape
    row = pl.BlockSpec((1, 1, D), lambda b, i: (b, 0, 0))
    return pl.pallas_call(
        functools.partial(_inproj_kernel, col_chunk=1152),
        grid=(bsz, seq // tm),
        in_specs=[
            pl.BlockSpec((1, tm, D), lambda b, i: (b, i, 0)),
            row, row,
            _const_spec((1, D)),
            _const_spec((D, W_IN_P)),
        ],
        out_specs=pl.BlockSpec((1, tm, W_IN_P), lambda b, i: (b, i, 0)),
        out_shape=jax.ShapeDtypeStruct((bsz, seq, W_IN_P), BF16),
        compiler_params=pltpu.CompilerParams(
            dimension_semantics=("arbitrary", "arbitrary"), vmem_limit_bytes=VMEM_LIMIT),
        name="in_proj",
    )(x, sc, sh, nw, w_in_p)


def _causal_conv(src_ref, buf_ref, w_ref, b_ref, c0, cw, ts):
    cs = slice(c0, c0 + cw)
    buf_ref[SUBLANES:SUBLANES + ts, cs] = src_ref[0, :, cs].astype(F32)
    acc = b_ref[:, cs] + w_ref[0:1, cs] * buf_ref[pl.ds(SUBLANES - 3, ts), cs]
    for k in range(1, CONV_K):
        acc = acc + w_ref[k:k + 1, cs] * buf_ref[pl.ds(SUBLANES - 3 + k, ts), cs]
    buf_ref[0:SUBLANES, cs] = buf_ref[ts:ts + SUBLANES, cs]
    return acc


def _mixer_kernel(x_ref, lx_ref, lg_ref, z_ref, xbc_ref, gt_ref, dt_ref,
                  g1_ref, post_ref, bgate_ref,
                  lcw_ref, lcb_ref, wg_ref, ba_ref, bx_ref, lam_ref, wpa_ref,
                  scw_ref, scb_ref, dtb_ref, alog_ref, dexp_ref, nw_ref, wpb_ref, wout_ref,
                  o_ref,
                  cbl, cbx, a_scr, u_scr, hc_scr, xs_scr, bb_scr, cc_scr, y_scr, st_scr,
                  *, ts):
    i = pl.program_id(1)

    @pl.when(i == 0)
    def _():
        cbl[0:SUBLANES, :] = jnp.zeros((SUBLANES, LRU_W), F32)
        cbx[0:SUBLANES, :] = jnp.zeros((SUBLANES, SSD_CONV), F32)
        hc_scr[...] = jnp.zeros_like(hc_scr)
        st_scr[...] = jnp.zeros_like(st_scr)

    sp = _softplus(-lam_ref[...])
    qw = 4 * LRU_HD
    for q in range(LRU_W // qw):
        cs = slice(q * qw, (q + 1) * qw)
        xq = _causal_conv(lx_ref, cbl, lcw_ref, lcb_ref, q * qw, qw, ts)
        pre = _dot(xq.astype(BF16), wg_ref[q])
        r = _sigmoid(pre[:, :qw] + ba_ref[:, cs])
        ig = _sigmoid(pre[:, qw:] + bx_ref[:, cs])
        log_a = (-LRU_C * r) * sp[:, cs]
        a = jnp.exp(log_a)
        a_scr[:, cs] = a
        u_scr[:, cs] = jnp.sqrt(1.0 - a * a) * (ig * xq)

    row = lax.broadcasted_iota(jnp.int32, (SUBLANES, LRU_W), 0)

    def scan_body(g, carry):
        r0 = pl.multiple_of(g * SUBLANES, SUBLANES)
        av = a_scr[pl.ds(r0, SUBLANES), :]
        bv = u_scr[pl.ds(r0, SUBLANES), :]
        for s in (1, 2, 4):
            keep = row >= s
            a_sh = jnp.where(keep, pltpu.roll(av, s, 0), 1.0)
            b_sh = jnp.where(keep, pltpu.roll(bv, s, 0), 0.0)
            bv = av * b_sh + bv
            av = av * a_sh
        h = bv + av * carry
        u_scr[pl.ds(r0, SUBLANES), :] = h
        return jnp.broadcast_to(h[SUBLANES - 1:SUBLANES, :], (SUBLANES, LRU_W))

    hc_scr[...] = lax.fori_loop(0, ts // SUBLANES, scan_body, hc_scr[...])

    ya_in = (u_scr[...] * _gelu_tanh(lg_ref[0].astype(F32))).astype(BF16)
    y_a = _dot(ya_in, wpa_ref[...])

    for g in range(SSD_G):
        v = _causal_conv(xbc_ref, cbx, scw_ref, scb_ref, g * SSD_GW, SSD_GW, ts)
        xs_scr[:, g * SSD_GW:(g + 1) * SSD_GW] = v * _sigmoid(v)
    for g in range(SSD_G):
        v = _causal_conv(xbc_ref, cbx, scw_ref, scb_ref, SSD_INNER + g * SSD_N, SSD_N, ts)
        bb_scr[:, g * SSD_N:(g + 1) * SSD_N] = v * _sigmoid(v)
        v = _causal_conv(xbc_ref, cbx, scw_ref, scb_ref, SSD_INNER + SSD_BC + g * SSD_N, SSD_N, ts)
        cc_scr[:, g * SSD_N:(g + 1) * SSD_N] = (v * _sigmoid(v)).astype(BF16)

    dt = _softplus(dt_ref[0].astype(F32) + dtb_ref[...])
    d_a = dt * (-jnp.exp(alog_ref[...]))

    li = lax.broadcasted_iota(jnp.int32, (SSD_L, SSD_L), 0)
    si = lax.broadcasted_iota(jnp.int32, (SSD_L, SSD_L), 1)
    causal = li >= si
    tri = causal.astype(BF16)
    expand = (lax.broadcasted_iota(jnp.int32, (HP, SSD_INNER), 0)
              == (lax.broadcasted_iota(jnp.int32, (HP, SSD_INNER), 1) // SSD_P)).astype(BF16)

    def per_head_to_channels(v):
        return sum(_dot(p, expand) for p in _split_bf16(v, 2))

    for c in range(ts // SSD_L):
        cr = slice(c * SSD_L, (c + 1) * SSD_L)
        csum = sum(_dot(tri, p) for p in _split_bf16(d_a[cr], 3))
        csum_t = csum.T
        last = csum[SSD_L - 1:SSD_L, :]
        dt_x = per_head_to_channels(dt[cr])
        w_x = per_head_to_channels(dt[cr] * jnp.exp(last - csum))
        ecs_x = per_head_to_channels(jnp.exp(csum))
        el_x = per_head_to_channels(jnp.broadcast_to(jnp.exp(last), (SUBLANES, HP)))[0:1, :]
        for g in range(SSD_G):
            gs = slice(g * SSD_GW, (g + 1) * SSD_GW)
            ns = slice(g * SSD_N, (g + 1) * SSD_N)
            xs_g = xs_scr[cr, gs]
            xc_g = (xs_g * dt_x[:, gs]).astype(BF16)
            xd_g = (xs_g * w_x[:, gs]).astype(BF16)
            c_g = cc_scr[cr, ns]
            b_t = bb_scr[cr, ns].T.astype(BF16)
            cb = _dot(c_g, b_t)
            s_g = st_scr[g]
            y_off = _dot(c_g, s_g.astype(BF16)) * ecs_x[:, gs]
            ys = []
            for e in range(SSD_HPG):
                hh = g * SSD_HPG + e
                seg = csum[:, hh:hh + 1] - csum_t[hh:hh + 1, :]
                dec = jnp.exp(jnp.where(causal, seg, -jnp.inf))
                m = (cb * dec).astype(BF16)
                ys.append(_dot(m, xc_g[:, e * SSD_P:(e + 1) * SSD_P]))
            y_scr[cr, gs] = jnp.concatenate(ys, axis=1) + y_off
            st_scr[g] = s_g * el_x[:, gs] + _dot(b_t, xd_g)

    y_b = jnp.zeros((ts, D), F32)
    for g in range(SSD_G):
        gs = slice(g * SSD_GW, (g + 1) * SSD_GW)
        yg = y_scr[:, gs] + dexp_ref[:, gs] * xs_scr[:, gs]
        zg = z_ref[0, :, gs].astype(F32)
        yg = yg * (zg * _sigmoid(zg))
        yn = (yg * _rms_scale(yg)) * nw_ref[:, gs]
        y_b = y_b + _dot(yn.astype(BF16), wpb_ref[gs, :])

    gt = gt_ref[0].astype(F32) + bgate_ref[...]
    merged = _sigmoid(gt[:, :D]) * y_a + _sigmoid(gt[:, D:]) * y_b
    y = _dot(merged.astype(BF16), wout_ref[...])
    o_ref[0] = x_ref[0] + g1_ref[0] * ((y * _rms_scale(y)) * post_ref[...])


def _mixer(x, proj, g1, post, bgate, lcw, lcb, wg, ba, bx, lam, wpa,
           scw, scb, dtb, alog, dexp, nw, wpb, wout, ts):
    bsz, seq, _ = x.shape

    def tile(width, col_block):
        return pl.BlockSpec((1, ts, width), lambda b, i: (b, i, col_block))

    in_specs = [
        tile(D, 0),
        tile(LRU_W, OFF_LX // LRU_W),
        tile(LRU_W, OFF_LG // LRU_W),
        tile(SSD_INNER, OFF_Z // SSD_INNER),
        tile(SSD_CONV, OFF_XBC // SSD_CONV),
        tile(2 * D, OFF_GT // (2 * D)),
        tile(HP, OFF_DT // HP),
        pl.BlockSpec((1, 1, D), lambda b, i: (b, 0, 0)),
    ]
    consts = [post, bgate, lcw, lcb, wg, ba, bx, lam, wpa, scw, scb, dtb, alog, dexp, nw, wpb, wout]
    in_specs += [_const_spec(a.shape) for a in consts]
    scratch = [
        pltpu.VMEM((ts + SUBLANES, LRU_W), F32),
        pltpu.VMEM((ts + SUBLANES, SSD_CONV), F32),
        pltpu.VMEM((ts, LRU_W), F32),
        pltpu.VMEM((ts, LRU_W), F32),
        pltpu.VMEM((SUBLANES, LRU_W), F32),
        pltpu.VMEM((ts, SSD_INNER), F32),
        pltpu.VMEM((ts, SSD_BC), F32),
        pltpu.VMEM((ts, SSD_BC), BF16),
        pltpu.VMEM((ts, SSD_INNER), F32),
        pltpu.VMEM((SSD_G, SSD_N, SSD_GW), F32),
    ]
    return pl.pallas_call(
        functools.partial(_mixer_kernel, ts=ts),
        grid=(bsz, seq // ts),
        in_specs=in_specs,
        out_specs=pl.BlockSpec((1, ts, D), lambda b, i: (b, i, 0)),
        out_shape=jax.ShapeDtypeStruct((bsz, seq, D), F32),
        scratch_shapes=scratch,
        compiler_params=pltpu.CompilerParams(
            dimension_semantics=("arbitrary", "arbitrary"), vmem_limit_bytes=VMEM_LIMIT),
        name="mixer",
    )(x, proj, proj, proj, proj, proj, proj, g1, *consts)


def _ffn_kernel(x_ref, sc_ref, sh_ref, g_ref, pre_ref, post_ref, w1_ref, w2_ref, o_ref, *, ff_chunk):
    x = x_ref[0]
    h = (x * _rms_scale(x)) * pre_ref[...]
    hb = (h * (1.0 + sc_ref[0]) + sh_ref[0]).astype(BF16)
    y = jnp.zeros(x.shape, F32)
    for j in range(D_FF // ff_chunk):
        sl = slice(j * ff_chunk, (j + 1) * ff_chunk)
        a = jnp.maximum(_dot(hb, w1_ref[:, sl]), 0.0)
        y = y + _dot((a * a).astype(BF16), w2_ref[sl, :])
    o_ref[0] = x + g_ref[0] * ((y * _rms_scale(y)) * post_ref[...])


def _ffn(x, sc, sh, g, pre, post, w1, w2, tm):
    bsz, seq, _ = x.shape
    row = pl.BlockSpec((1, 1, D), lambda b, i: (b, 0, 0))
    return pl.pallas_call(
        functools.partial(_ffn_kernel, ff_chunk=1024),
        grid=(bsz, seq // tm),
        in_specs=[
            pl.BlockSpec((1, tm, D), lambda b, i: (b, i, 0)),
            row, row, row,
            _const_spec((1, D)), _const_spec((1, D)),
            _const_spec((D, D_FF)), _const_spec((D_FF, D)),
        ],
        out_specs=pl.BlockSpec((1, tm, D), lambda b, i: (b, i, 0)),
        out_shape=jax.ShapeDtypeStruct((bsz, seq, D), F32),
        compiler_params=pltpu.CompilerParams(
            dimension_semantics=("arbitrary", "arbitrary"), vmem_limit_bytes=VMEM_LIMIT),
        name="ffn",
    )(x, sc, sh, g, pre, post, w1, w2)


def _block_diag4(w):
    w4 = w.reshape(LRU_HEADS // 4, 4, LRU_HD, LRU_HD)
    eye = jnp.eye(4, dtype=w.dtype)
    return jnp.einsum('qeij,ef->qeifj', w4, eye).reshape(LRU_HEADS // 4, 4 * LRU_HD, 4 * LRU_HD)


def _pad_heads(v, fill=0.0):
    return jnp.pad(v.reshape(1, SSD_H), ((0, 0), (0, HP - SSD_H)), constant_values=fill)


def kernel(x, c, w_ada, b_ada, pre_norm1, post_norm1, w_in, b_gate, lru_conv_w, lru_conv_b,
           lru_wa, lru_ba, lru_wx, lru_bx, lru_lambda, w_pa, ssd_conv_w, ssd_conv_b,
           ssd_dt_bias, ssd_a_log, ssd_d, ssd_norm_w, w_pb, w_out, pre_norm2, post_norm2,
           w_ff1, w_ff2):
    depth = w_ada.shape[0]
    bsz, seq, _ = x.shape
    ts = 256 if seq % 256 == 0 else SSD_L
    tm = 512 if seq % 512 == 0 else ts
    off_dt_src = OFF_XBC + SSD_CONV
    for l in range(depth):
        mod = _adaln(c, w_ada[l], b_ada[l]).reshape(bsz, N_MOD, 1, D)
        sh1, sc1, g1, sh2, sc2, g2 = (mod[:, k] for k in range(N_MOD))
        w = w_in[l]
        w_in_p = jnp.concatenate(
            [w[:, :off_dt_src], w[:, off_dt_src + SSD_H:], w[:, off_dt_src:off_dt_src + SSD_H],
             jnp.zeros((D, HP - SSD_H), w.dtype)], axis=1).astype(BF16)
        proj = _in_proj(x, sc1, sh1, pre_norm1[l].reshape(1, D), w_in_p, tm)
        wg = jnp.concatenate([_block_diag4(lru_wa[l]), _block_diag4(lru_wx[l])], axis=2).astype(BF16)
        x = _mixer(
            x, proj, g1, post_norm1[l].reshape(1, D), b_gate[l].reshape(1, 2 * D),
            lru_conv_w[l], lru_conv_b[l].reshape(1, LRU_W), wg,
            lru_ba[l].reshape(1, LRU_W), lru_bx[l].reshape(1, LRU_W), lru_lambda[l].reshape(1, LRU_W),
            w_pa[l].astype(BF16),
            ssd_conv_w[l], ssd_conv_b[l].reshape(1, SSD_CONV),
            _pad_heads(ssd_dt_bias[l]), _pad_heads(ssd_a_log[l]),
            jnp.repeat(ssd_d[l], SSD_P).reshape(1, SSD_INNER), ssd_norm_w[l].reshape(1, SSD_INNER),
            w_pb[l].astype(BF16), w_out[l].astype(BF16), ts)
        x = _ffn(x, sc2, sh2, g2, pre_norm2[l].reshape(1, D), post_norm2[l].reshape(1, D),
                 w_ff1[l].astype(BF16), w_ff2[l].astype(BF16), tm)
    return x
```

```python
import functools
import math

import jax
import jax.numpy as jnp
from jax import lax
from jax.experimental import pallas as pl
from jax.experimental.pallas import tpu as pltpu

F32 = jnp.float32
BF16 = jnp.bfloat16

D = 1024
LRU_W = D
LRU_HEADS = 16
LRU_HD = LRU_W // LRU_HEADS
LRU_C = 8.0
CONV_K = 4
SSD_INNER = 2 * D
SSD_P = 64
SSD_H = SSD_INNER // SSD_P
SSD_G = 8
SSD_HPG = SSD_H // SSD_G
SSD_N = 128
SSD_L = 128
SSD_GW = SSD_HPG * SSD_P
SSD_BC = SSD_G * SSD_N
SSD_CONV = SSD_INNER + 2 * SSD_BC
D_FF = 4 * D
N_MOD = 6
EPS = 1e-6

LANES = 128
SUBLANES = 8
HP = LANES
OFF_XBC, OFF_LX = 0, SSD_CONV
N_CONV = SSD_CONV + LRU_W
OFF_LG = N_CONV
OFF_Z = OFF_LG + LRU_W
OFF_GT = OFF_Z + SSD_INNER
OFF_DT = OFF_GT + 2 * D
W_IN_P = OFF_DT + HP
W_ACT = OFF_DT
COL_CHUNK = 512
ROW_BLOCK = 64

VMEM_LIMIT = 56 * 1024 * 1024


def _sigmoid(v):
    return 0.5 * jnp.tanh(0.5 * v) + 0.5


def _softplus(v):
    return jnp.maximum(v, 0.0) + jnp.log1p(jnp.exp(-jnp.abs(v)))


def _gelu_tanh(v):
    c = math.sqrt(2.0 / math.pi)
    return 0.5 * v * (1.0 + jnp.tanh(c * (v + 0.044715 * (v * v * v))))


def _rms_scale(v):
    return lax.rsqrt(jnp.mean(v * v, axis=-1, keepdims=True) + EPS)


def _split_bf16(v, n):
    parts = []
    r = v
    for _ in range(n):
        p = r.astype(BF16)
        parts.append(p)
        r = r - p.astype(F32)
    return parts


def _dot(a, b):
    return jnp.dot(a, b, preferred_element_type=F32)


def _adaln_kernel(c_ref, w_ref, b_ref, o_ref):
    c = c_ref[...]
    ca = (c * _sigmoid(c)).astype(BF16)
    o_ref[...] = _dot(ca, w_ref[...].astype(BF16)) + b_ref[...]


def _adaln(c, w_ada, b_ada):
    bsz = c.shape[0]
    n = w_ada.shape[1]
    tn = D
    return pl.pallas_call(
        _adaln_kernel,
        grid=(n // tn,),
        in_specs=[
            pl.BlockSpec((bsz, D), lambda j: (0, 0)),
            pl.BlockSpec((D, tn), lambda j: (0, j)),
            pl.BlockSpec((1, tn), lambda j: (0, j)),
        ],
        out_specs=pl.BlockSpec((bsz, tn), lambda j: (0, j)),
        out_shape=jax.ShapeDtypeStruct((bsz, n), F32),
        name="adaln_mod",
    )(c, w_ada, b_ada.reshape(1, n))


def _inproj_kernel(x_ref, sc_ref, sh_ref, nw_ref, w_ref, cw_ref, cb_ref, bgate_ref, dtb_ref,
                   o_ref, dt_ref, win, stg, tail, *, tm):
    @pl.when(pl.program_id(1) == 0)
    def _():
        tail[...] = jnp.zeros_like(tail)

    x = x_ref[0]
    h = (x * _rms_scale(x)) * nw_ref[...]
    h = h * (1.0 + sc_ref[0]) + sh_ref[0]
    hb = h.astype(BF16)
    n_slab = COL_CHUNK // LANES
    half = ROW_BLOCK // 2

    for j in range(W_ACT // COL_CHUNK):
        c0 = j * COL_CHUNK
        pbuf = win.at[j % 2]
        for s in range(n_slab):
            ls = slice(c0 + s * LANES, c0 + (s + 1) * LANES)
            if c0 < N_CONV:
                pbuf[s, 0:SUBLANES, :] = tail[:, ls]
        p = _dot(hb, w_ref[:, c0:c0 + COL_CHUNK])
        for s in range(n_slab):
            ls = slice(c0 + s * LANES, c0 + (s + 1) * LANES)
            pbuf[s, SUBLANES:SUBLANES + tm, :] = p[:, s * LANES:(s + 1) * LANES]
            if c0 < N_CONV:
                tail[:, ls] = p[tm - SUBLANES:, s * LANES:(s + 1) * LANES]
        for s in range(n_slab):
            ls = slice(c0 + s * LANES, c0 + (s + 1) * LANES)
            for r0 in range(0, tm, ROW_BLOCK):
                if c0 < N_CONV:
                    taps = [pbuf[s, pl.ds(SUBLANES + r0 - 3 + m, half, stride=2), :] for m in range(5)]
                    outs = []
                    for par in range(2):
                        v = cb_ref[:, ls] + cw_ref[0:1, ls] * taps[par]
                        for k in range(1, CONV_K):
                            v = v + cw_ref[k:k + 1, ls] * taps[par + k]
                        if c0 < SSD_CONV:
                            v = v * _sigmoid(v)
                        outs.append(v)
                    stg[s, pl.ds(r0, half, stride=2), :] = outs[0]
                    stg[s, pl.ds(r0 + 1, half, stride=2), :] = outs[1]
                    v = stg[s, r0:r0 + ROW_BLOCK, :]
                else:
                    v = pbuf[s, SUBLANES + r0:SUBLANES + r0 + ROW_BLOCK, :]
                    if c0 < OFF_Z:
                        v = _gelu_tanh(v)
                    elif c0 < OFF_GT:
                        v = v * _sigmoid(v)
                    else:
                        v = _sigmoid(v + bgate_ref[:, c0 - OFF_GT + s * LANES:c0 - OFF_GT + (s + 1) * LANES])
                o_ref[0, r0:r0 + ROW_BLOCK, ls] = v.astype(BF16)
    dt_ref[0] = _softplus(_dot(hb, w_ref[:, OFF_DT:OFF_DT + HP]) + dtb_ref[...])


def _const_spec(shape):
    nd = len(shape)
    return pl.BlockSpec(shape, lambda b, i: (0,) * nd, pipeline_mode=pl.Buffered(1))


def _in_proj(x, sc, sh, nw, w_in_p, conv_w, conv_b, bgate, dtb, tm):
    bsz, seq, _ = x.shape
    row = pl.BlockSpec((1, 1, D), lambda b, i: (b, 0, 0))
    consts = [nw, w_in_p, conv_w, conv_b, bgate, dtb]
    return pl.pallas_call(
        functools.partial(_inproj_kernel, tm=tm),
        grid=(bsz, seq // tm),
        in_specs=[pl.BlockSpec((1, tm, D), lambda b, i: (b, i, 0)), row, row]
        + [_const_spec(a.shape) for a in consts],
        out_specs=[pl.BlockSpec((1, tm, W_ACT), lambda b, i: (b, i, 0)),
                   pl.BlockSpec((1, tm, HP), lambda b, i: (b, i, 0))],
        out_shape=[jax.ShapeDtypeStruct((bsz, seq, W_ACT), BF16),
                   jax.ShapeDtypeStruct((bsz, seq, HP), F32)],
        scratch_shapes=[
            pltpu.VMEM((2, COL_CHUNK // LANES, SUBLANES + tm, LANES), F32),
            pltpu.VMEM((COL_CHUNK // LANES, tm, LANES), F32),
            pltpu.VMEM((SUBLANES, N_CONV), F32)],
        compiler_params=pltpu.CompilerParams(
            dimension_semantics=("arbitrary", "arbitrary"), vmem_limit_bytes=VMEM_LIMIT),
        name="in_proj",
    )(x, sc, sh, *consts)


def _mixer_kernel(x_ref, lx_ref, lg_ref, z_ref, xbc_ref, gt_ref, dt_ref,
                  g1_ref, post_ref,
                  wg_ref, ba_ref, bx_ref, lam_ref, wpa_ref,
                  alog_ref, dexp_ref, nw_ref, wpb_ref, wout_ref,
                  o_ref,
                  a_scr, u_scr, hc_scr, y_scr, st_scr,
                  *, ts):
    @pl.when(pl.program_id(1) == 0)
    def _():
        hc_scr[...] = jnp.zeros_like(hc_scr)
        st_scr[...] = jnp.zeros_like(st_scr)

    sp = _softplus(-lam_ref[...])
    qw = 4 * LRU_HD
    for q in range(LRU_W // qw):
        cs = slice(q * qw, (q + 1) * qw)
        xq = lx_ref[0, :, cs]
        pre = _dot(xq, wg_ref[q])
        r = _sigmoid(pre[:, :qw] + ba_ref[:, cs])
        ig = _sigmoid(pre[:, qw:] + bx_ref[:, cs])
        log_a = (-LRU_C * r) * sp[:, cs]
        a = jnp.exp(log_a)
        a_scr[:, cs] = a
        u_scr[:, cs] = jnp.sqrt(1.0 - a * a) * (ig * xq.astype(F32))

    row = lax.broadcasted_iota(jnp.int32, (SUBLANES, LRU_W), 0)

    def scan_body(g, carry):
        r0 = pl.multiple_of(g * SUBLANES, SUBLANES)
        av = a_scr[pl.ds(r0, SUBLANES), :]
        bv = u_scr[pl.ds(r0, SUBLANES), :]
        for s in (1, 2, 4):
            keep = row >= s
            a_sh = jnp.where(keep, pltpu.roll(av, s, 0), 1.0)
            b_sh = jnp.where(keep, pltpu.roll(bv, s, 0), 0.0)
            bv = av * b_sh + bv
            av = av * a_sh
        h = bv + av * carry
        u_scr[pl.ds(r0, SUBLANES), :] = h
        return jnp.broadcast_to(h[SUBLANES - 1:SUBLANES, :], (SUBLANES, LRU_W))

    hc_scr[...] = lax.fori_loop(0, ts // SUBLANES, scan_body, hc_scr[...])

    ya_in = (u_scr[...] * lg_ref[0].astype(F32)).astype(BF16)
    y_a = _dot(ya_in, wpa_ref[...])

    dt = dt_ref[0]
    d_a = dt * (-jnp.exp(alog_ref[...]))

    li = lax.broadcasted_iota(jnp.int32, (SSD_L, SSD_L), 0)
    si = lax.broadcasted_iota(jnp.int32, (SSD_L, SSD_L), 1)
    causal = li >= si
    tri = causal.astype(BF16)
    expand = (lax.broadcasted_iota(jnp.int32, (HP, SSD_INNER), 0)
              == (lax.broadcasted_iota(jnp.int32, (HP, SSD_INNER), 1) // SSD_P)).astype(BF16)

    def per_head_to_channels(v):
        return sum(_dot(p, expand) for p in _split_bf16(v, 2))

    for c in range(ts // SSD_L):
        cr = slice(c * SSD_L, (c + 1) * SSD_L)
        csum = sum(_dot(tri, p) for p in _split_bf16(d_a[cr], 3))
        csum_t = csum.T
        last = csum[SSD_L - 1:SSD_L, :]
        dt_x = per_head_to_channels(dt[cr])
        w_x = per_head_to_channels(dt[cr] * jnp.exp(last - csum))
        ecs_x = per_head_to_channels(jnp.exp(csum))
        el_x = per_head_to_channels(jnp.broadcast_to(jnp.exp(last), (SUBLANES, HP)))[0:1, :]
        for g in range(SSD_G):
            gs = slice(g * SSD_GW, (g + 1) * SSD_GW)
            xs_g = xbc_ref[0, cr, gs].astype(F32)
            xc_g = (xs_g * dt_x[:, gs]).astype(BF16)
            xd_g = (xs_g * w_x[:, gs]).astype(BF16)
            b_g = xbc_ref[0, cr, SSD_INNER + g * SSD_N:SSD_INNER + (g + 1) * SSD_N]
            c_g = xbc_ref[0, cr, SSD_INNER + SSD_BC + g * SSD_N:SSD_INNER + SSD_BC + (g + 1) * SSD_N]
            b_t = b_g.astype(F32).T.astype(BF16)
            cb = _dot(c_g, b_t)
            s_g = st_scr[g]
            y_off = _dot(c_g, s_g.astype(BF16)) * ecs_x[:, gs]
            ys = []
            for e in range(SSD_HPG):
                hh = g * SSD_HPG + e
                seg = csum[:, hh:hh + 1] - csum_t[hh:hh + 1, :]
                dec = jnp.exp(jnp.where(causal, seg, -jnp.inf))
                m = (cb * dec).astype(BF16)
                ys.append(_dot(m, xc_g[:, e * SSD_P:(e + 1) * SSD_P]))
            y_scr[cr, gs] = jnp.concatenate(ys, axis=1) + y_off
            st_scr[g] = s_g * el_x[:, gs] + _dot(b_t, xd_g)

    y_b = jnp.zeros((ts, D), F32)
    for g in range(SSD_G):
        gs = slice(g * SSD_GW, (g + 1) * SSD_GW)
        yg = y_scr[:, gs] + dexp_ref[:, gs] * xbc_ref[0, :, gs].astype(F32)
        yg = yg * z_ref[0, :, gs].astype(F32)
        yn = (yg * _rms_scale(yg)) * nw_ref[:, gs]
        y_b = y_b + _dot(yn.astype(BF16), wpb_ref[gs, :])

    merged = gt_ref[0, :, :D].astype(F32) * y_a + gt_ref[0, :, D:].astype(F32) * y_b
    y = _dot(merged.astype(BF16), wout_ref[...])
    o_ref[0] = x_ref[0] + g1_ref[0] * ((y * _rms_scale(y)) * post_ref[...])


def _mixer(x, act, dt, g1, post, wg, ba, bx, lam, wpa, alog, dexp, nw, wpb, wout, ts):
    bsz, seq, _ = x.shape

    def tile(width, col_off):
        return pl.BlockSpec((1, ts, width), lambda b, i: (b, i, col_off // width))

    in_specs = [
        tile(D, 0),
        tile(LRU_W, OFF_LX),
        tile(LRU_W, OFF_LG),
        tile(SSD_INNER, OFF_Z),
        tile(SSD_CONV, OFF_XBC),
        tile(2 * D, OFF_GT),
        tile(HP, 0),
        pl.BlockSpec((1, 1, D), lambda b, i: (b, 0, 0)),
    ]
    consts = [post, wg, ba, bx, lam, wpa, alog, dexp, nw, wpb, wout]
    in_specs += [_const_spec(a.shape) for a in consts]
    scratch = [
        pltpu.VMEM((ts, LRU_W), F32),
        pltpu.VMEM((ts, LRU_W), F32),
        pltpu.VMEM((SUBLANES, LRU_W), F32),
        pltpu.VMEM((ts, SSD_INNER), F32),
        pltpu.VMEM((SSD_G, SSD_N, SSD_GW), F32),
    ]
    return pl.pallas_call(
        functools.partial(_mixer_kernel, ts=ts),
        grid=(bsz, seq // ts),
        in_specs=in_specs,
        out_specs=pl.BlockSpec((1, ts, D), lambda b, i: (b, i, 0)),
        out_shape=jax.ShapeDtypeStruct((bsz, seq, D), F32),
        scratch_shapes=scratch,
        compiler_params=pltpu.CompilerParams(
            dimension_semantics=("arbitrary", "arbitrary"), vmem_limit_bytes=VMEM_LIMIT),
        name="mixer",
    )(x, act, act, act, act, act, dt, g1, *consts)


def _ffn_kernel(x_ref, sc_ref, sh_ref, g_ref, pre_ref, post_ref, w1_ref, w2_ref, o_ref, *, ff_chunk):
    x = x_ref[0]
    h = (x * _rms_scale(x)) * pre_ref[...]
    hb = (h * (1.0 + sc_ref[0]) + sh_ref[0]).astype(BF16)
    y = jnp.zeros(x.shape, F32)
    for j in range(D_FF // ff_chunk):
        sl = slice(j * ff_chunk, (j + 1) * ff_chunk)
        a = jnp.maximum(_dot(hb, w1_ref[:, sl]), 0.0)
        y = y + _dot((a * a).astype(BF16), w2_ref[sl, :])
    o_ref[0] = x + g_ref[0] * ((y * _rms_scale(y)) * post_ref[...])


def _ffn(x, sc, sh, g, pre, post, w1, w2, tm):
    bsz, seq, _ = x.shape
    row = pl.BlockSpec((1, 1, D), lambda b, i: (b, 0, 0))
    return pl.pallas_call(
        functools.partial(_ffn_kernel, ff_chunk=1024),
        grid=(bsz, seq // tm),
        in_specs=[
            pl.BlockSpec((1, tm, D), lambda b, i: (b, i, 0)),
            row, row, row,
            _const_spec((1, D)), _const_spec((1, D)),
            _const_spec((D, D_FF)), _const_spec((D_FF, D)),
        ],
        out_specs=pl.BlockSpec((1, tm, D), lambda b, i: (b, i, 0)),
        out_shape=jax.ShapeDtypeStruct((bsz, seq, D), F32),
        compiler_params=pltpu.CompilerParams(
            dimension_semantics=("arbitrary", "arbitrary"), vmem_limit_bytes=VMEM_LIMIT),
        name="ffn",
    )(x, sc, sh, g, pre, post, w1, w2)


def _block_diag4(w):
    w4 = w.reshape(LRU_HEADS // 4, 4, LRU_HD, LRU_HD)
    eye = jnp.eye(4, dtype=w.dtype)
    return jnp.einsum('qeij,ef->qeifj', w4, eye).reshape(LRU_HEADS // 4, 4 * LRU_HD, 4 * LRU_HD)


def _pad_heads(v, fill=0.0):
    return jnp.pad(v.reshape(1, SSD_H), ((0, 0), (0, HP - SSD_H)), constant_values=fill)


def _permute_w_in(w):
    o_lg, o_z, o_xbc = LRU_W, 2 * LRU_W, 2 * LRU_W + SSD_INNER
    o_dt = o_xbc + SSD_CONV
    o_gt = o_dt + SSD_H
    return jnp.concatenate(
        [w[:, o_xbc:o_dt], w[:, :o_lg], w[:, o_lg:o_z], w[:, o_z:o_xbc], w[:, o_gt:], w[:, o_dt:o_gt],
         jnp.zeros((D, HP - SSD_H), w.dtype)], axis=1).astype(BF16)


def kernel(x, c, w_ada, b_ada, pre_norm1, post_norm1, w_in, b_gate, lru_conv_w, lru_conv_b,
           lru_wa, lru_ba, lru_wx, lru_bx, lru_lambda, w_pa, ssd_conv_w, ssd_conv_b,
           ssd_dt_bias, ssd_a_log, ssd_d, ssd_norm_w, w_pb, w_out, pre_norm2, post_norm2,
           w_ff1, w_ff2):
    depth = w_ada.shape[0]
    bsz, seq, _ = x.shape
    ts = 256 if seq % 256 == 0 else SSD_L
    tm = 512 if seq % 512 == 0 else ts
    for l in range(depth):
        mod = _adaln(c, w_ada[l], b_ada[l]).reshape(bsz, N_MOD, 1, D)
        sh1, sc1, g1, sh2, sc2, g2 = (mod[:, k] for k in range(N_MOD))
        conv_w = jnp.concatenate([ssd_conv_w[l], lru_conv_w[l]], axis=1)
        conv_b = jnp.concatenate([ssd_conv_b[l], lru_conv_b[l]]).reshape(1, N_CONV)
        act, dt = _in_proj(x, sc1, sh1, pre_norm1[l].reshape(1, D), _permute_w_in(w_in[l]),
                           conv_w, conv_b, b_gate[l].reshape(1, 2 * D), _pad_heads(ssd_dt_bias[l]), tm)
        wg = jnp.concatenate([_block_diag4(lru_wa[l]), _block_diag4(lru_wx[l])], axis=2).astype(BF16)
        x = _mixer(
            x, act, dt, g1, post_norm1[l].reshape(1, D), wg,
            lru_ba[l].reshape(1, LRU_W), lru_bx[l].reshape(1, LRU_W), lru_lambda[l].reshape(1, LRU_W),
            w_pa[l].astype(BF16), _pad_heads(ssd_a_log[l]),
            jnp.repeat(ssd_d[l], SSD_P).reshape(1, SSD_INNER), ssd_norm_w[l].reshape(1, SSD_INNER),
            w_pb[l].astype(BF16), w_out[l].astype(BF16), ts)
        x = _ffn(x, sc2, sh2, g2, pre_norm2[l].reshape(1, D), post_norm2[l].reshape(1, D),
                 w_ff1[l].astype(BF16), w_ff2[l].astype(BF16), tm)
    return x
```

```python
import functools
import math

import jax
import jax.numpy as jnp
from jax import lax
from jax.experimental import pallas as pl
from jax.experimental.pallas import tpu as pltpu

F32 = jnp.float32
BF16 = jnp.bfloat16

D = 1024
LRU_W = D
LRU_HEADS = 16
LRU_HD = LRU_W // LRU_HEADS
LRU_C = 8.0
CONV_K = 4
SSD_INNER = 2 * D
SSD_P = 64
SSD_H = SSD_INNER // SSD_P
SSD_G = 8
SSD_HPG = SSD_H // SSD_G
SSD_N = 128
SSD_L = 128
SSD_GW = SSD_HPG * SSD_P
SSD_BC = SSD_G * SSD_N
SSD_CONV = SSD_INNER + 2 * SSD_BC
D_FF = 4 * D
N_MOD = 6
EPS = 1e-6
LOG2E = math.log2(math.e)

LANES = 128
SUBLANES = 8
HP = LANES
OFF_XBC, OFF_LX = 0, SSD_CONV
N_CONV = SSD_CONV + LRU_W
OFF_LG = N_CONV
OFF_Z = OFF_LG + LRU_W
OFF_GT = OFF_Z + SSD_INNER
OFF_DT = OFF_GT + 2 * D
W_IN_P = OFF_DT + HP
W_ACT = OFF_DT
COL_CHUNK = 512
ROW_BLOCK = 64
LRU_ROWS = 64
FF_CHUNK = 1024

VMEM_LIMIT = 56 * 1024 * 1024


def _sigmoid(v):
    return 0.5 * jnp.tanh(0.5 * v) + 0.5


def _softplus(v):
    return jnp.maximum(v, 0.0) + jnp.log1p(jnp.exp(-jnp.abs(v)))


def _gelu_tanh(v):
    c = math.sqrt(2.0 / math.pi)
    return 0.5 * v * (1.0 + jnp.tanh(c * (v + 0.044715 * (v * v * v))))


def _rms_scale(v):
    return lax.rsqrt(jnp.mean(v * v, axis=-1, keepdims=True) + EPS)


def _split_bf16(v, n):
    parts = []
    r = v
    for _ in range(n):
        p = r.astype(BF16)
        parts.append(p)
        r = r - p.astype(F32)
    return parts


def _dot(a, b):
    return jnp.dot(a, b, preferred_element_type=F32)


def _adaln_kernel(c_ref, w_ref, b_ref, o_ref):
    c = c_ref[...]
    ca = (c * _sigmoid(c)).astype(BF16)
    o_ref[...] = _dot(ca, w_ref[...].astype(BF16)) + b_ref[...]


def _adaln(c, w_ada, b_ada):
    bsz = c.shape[0]
    n = w_ada.shape[1]
    tn = D
    return pl.pallas_call(
        _adaln_kernel,
        grid=(n // tn,),
        in_specs=[
            pl.BlockSpec((bsz, D), lambda j: (0, 0)),
            pl.BlockSpec((D, tn), lambda j: (0, j)),
            pl.BlockSpec((1, tn), lambda j: (0, j)),
        ],
        out_specs=pl.BlockSpec((bsz, tn), lambda j: (0, j)),
        out_shape=jax.ShapeDtypeStruct((bsz, n), F32),
        name="adaln_mod",
    )(c, w_ada, b_ada.reshape(1, n))


def _inproj_kernel(x_ref, sc_ref, sh_ref, nw_ref, w_ref, cw_ref, cb_ref, bgate_ref, dtb_ref,
                   o_ref, dt_ref, win, stg, tail, *, tm):
    @pl.when(pl.program_id(1) == 0)
    def _():
        tail[...] = jnp.zeros_like(tail)

    x = x_ref[0]
    h = (x * _rms_scale(x)) * nw_ref[...]
    h = h * (1.0 + sc_ref[0]) + sh_ref[0]
    hb = h.astype(BF16)
    n_slab = COL_CHUNK // LANES
    half = ROW_BLOCK // 2

    for j in range(W_ACT // COL_CHUNK):
        c0 = j * COL_CHUNK
        pbuf = win.at[j % 2]
        for s in range(n_slab):
            ls = slice(c0 + s * LANES, c0 + (s + 1) * LANES)
            if c0 < N_CONV:
                pbuf[s, 0:SUBLANES, :] = tail[:, ls]
        p = _dot(hb, w_ref[:, c0:c0 + COL_CHUNK])
        for s in range(n_slab):
            ls = slice(c0 + s * LANES, c0 + (s + 1) * LANES)
            pbuf[s, SUBLANES:SUBLANES + tm, :] = p[:, s * LANES:(s + 1) * LANES]
            if c0 < N_CONV:
                tail[:, ls] = p[tm - SUBLANES:, s * LANES:(s + 1) * LANES]
        for s in range(n_slab):
            ls = slice(c0 + s * LANES, c0 + (s + 1) * LANES)
            for r0 in range(0, tm, ROW_BLOCK):
                if c0 < N_CONV:
                    taps = [pbuf[s, pl.ds(SUBLANES + r0 - 3 + m, half, stride=2), :] for m in range(5)]
                    outs = []
                    for par in range(2):
                        v = cb_ref[:, ls] + cw_ref[0:1, ls] * taps[par]
                        for k in range(1, CONV_K):
                            v = v + cw_ref[k:k + 1, ls] * taps[par + k]
                        if c0 < SSD_CONV:
                            v = v * _sigmoid(v)
                        outs.append(v)
                    stg[s, pl.ds(r0, half, stride=2), :] = outs[0]
                    stg[s, pl.ds(r0 + 1, half, stride=2), :] = outs[1]
                    v = stg[s, r0:r0 + ROW_BLOCK, :]
                else:
                    v = pbuf[s, SUBLANES + r0:SUBLANES + r0 + ROW_BLOCK, :]
                    if c0 < OFF_Z:
                        v = _gelu_tanh(v)
                    elif c0 < OFF_GT:
                        v = v * _sigmoid(v)
                    else:
                        v = _sigmoid(v + bgate_ref[:, c0 - OFF_GT + s * LANES:c0 - OFF_GT + (s + 1) * LANES])
                o_ref[0, r0:r0 + ROW_BLOCK, ls] = v.astype(BF16)
    dt_ref[0] = _softplus(_dot(hb, w_ref[:, OFF_DT:OFF_DT + HP]) + dtb_ref[...])


def _const_spec(shape):
    nd = len(shape)
    return pl.BlockSpec(shape, lambda b, i: (0,) * nd, pipeline_mode=pl.Buffered(1))


def _in_proj(x, sc, sh, nw, w_in_p, conv_w, conv_b, bgate, dtb, tm):
    bsz, seq, _ = x.shape
    row = pl.BlockSpec((1, 1, D), lambda b, i: (b, 0, 0))
    consts = [nw, w_in_p, conv_w, conv_b, bgate, dtb]
    return pl.pallas_call(
        functools.partial(_inproj_kernel, tm=tm),
        grid=(bsz, seq // tm),
        in_specs=[pl.BlockSpec((1, tm, D), lambda b, i: (b, i, 0)), row, row]
        + [_const_spec(a.shape) for a in consts],
        out_specs=[pl.BlockSpec((1, tm, W_ACT), lambda b, i: (b, i, 0)),
                   pl.BlockSpec((1, tm, HP), lambda b, i: (b, i, 0))],
        out_shape=[jax.ShapeDtypeStruct((bsz, seq, W_ACT), BF16),
                   jax.ShapeDtypeStruct((bsz, seq, HP), F32)],
        scratch_shapes=[
            pltpu.VMEM((2, COL_CHUNK // LANES, SUBLANES + tm, LANES), F32),
            pltpu.VMEM((COL_CHUNK // LANES, tm, LANES), F32),
            pltpu.VMEM((SUBLANES, N_CONV), F32)],
        compiler_params=pltpu.CompilerParams(
            dimension_semantics=("arbitrary", "arbitrary"), vmem_limit_bytes=VMEM_LIMIT),
        name="in_proj",
    )(x, sc, sh, *consts)


def _mixer_ffn_kernel(x_ref, lx_ref, lg_ref, z_ref, xbc_ref, gt_ref, dt_ref,
                      g1_ref, sc2_ref, sh2_ref, g2_ref,
                      post1_ref, wg_ref, ba_ref, bx_ref, lam_ref, wpa_ref,
                      alog_ref, dexp_ref, nw_ref, wpb_ref, wout_ref,
                      pre2_ref, post2_ref, w1_ref, w2_ref,
                      o_ref,
                      pre_scr, ya_scr, hc_scr, yn_scr, st_scr, x1_scr, act_scr,
                      *, ts, nt):
    s = pl.program_id(0)
    cur = s % 2

    @pl.when(s == 0)
    def _():
        x1_scr[...] = jnp.zeros_like(x1_scr)

    @pl.when(s % nt == 0)
    def _():
        hc_scr[...] = jnp.zeros_like(hc_scr)
        st_scr[...] = jnp.zeros_like(st_scr)

    def ffn_pre():
        xp = x1_scr[1 - cur]
        h = (xp * _rms_scale(xp)) * pre2_ref[...]
        ffn["hb"] = (h * (1.0 + sc2_ref[0]) + sh2_ref[0]).astype(BF16)

    def ffn_up(j):
        sl = slice(j * FF_CHUNK, (j + 1) * FF_CHUNK)
        a = jnp.maximum(_dot(ffn["hb"], w1_ref[:, sl]), 0.0)
        act_scr[:, sl] = (a * a).astype(BF16)

    def ffn_down(j):
        sl = slice(j * FF_CHUNK, (j + 1) * FF_CHUNK)
        part = _dot(act_scr[:, sl], w2_ref[sl, :])
        ffn["y"] = part if j == 0 else ffn["y"] + part

    def ffn_post():
        y = ffn["y"]
        o_ref[0] = x1_scr[1 - cur] + g2_ref[0] * ((y * _rms_scale(y)) * post2_ref[...])

    ffn = {}
    ffn_tasks = [ffn_pre]
    for j in range(D_FF // FF_CHUNK):
        ffn_tasks += [functools.partial(ffn_up, j), functools.partial(ffn_down, j)]
    ffn_tasks.append(ffn_post)

    sp = _softplus(-lam_ref[...])
    qw = 4 * LRU_HD
    n_q = LRU_W // qw
    row = lax.broadcasted_iota(jnp.int32, (SUBLANES, qw), 0)
    carries = [hc_scr[:, q * qw:(q + 1) * qw] for q in range(n_q)]

    def lru_gates(q):
        pre_scr[q] = _dot(lx_ref[0, :, q * qw:(q + 1) * qw], wg_ref[q])

    def lru_block(q, rb):
        cs = slice(q * qw, (q + 1) * qw)
        rs = slice(rb * LRU_ROWS, (rb + 1) * LRU_ROWS)
        r = _sigmoid(pre_scr[q, rs, :qw] + ba_ref[:, cs])
        ig = _sigmoid(pre_scr[q, rs, qw:] + bx_ref[:, cs])
        a = jnp.exp((-LRU_C * r) * sp[:, cs])
        u = jnp.sqrt(1.0 - a * a) * (ig * lx_ref[0, rs, cs].astype(F32))
        carry = carries[q]
        hs = []
        for k in range(LRU_ROWS // SUBLANES):
            av = a[k * SUBLANES:(k + 1) * SUBLANES]
            bv = u[k * SUBLANES:(k + 1) * SUBLANES]
            for s in (1, 2, 4):
                keep = row >= s
                a_sh = jnp.where(keep, pltpu.roll(av, s, 0), 1.0)
                b_sh = jnp.where(keep, pltpu.roll(bv, s, 0), 0.0)
                bv = av * b_sh + bv
                av = av * a_sh
            h = bv + av * carry
            hs.append(h)
            carry = jnp.broadcast_to(h[SUBLANES - 1:SUBLANES, :], (SUBLANES, qw))
        carries[q] = carry
        ya_scr[rs, cs] = (jnp.concatenate(hs, axis=0) * lg_ref[0, rs, cs].astype(F32)).astype(BF16)

    lru_tasks = []
    for q in range(n_q):
        lru_tasks.append(functools.partial(lru_gates, q))
        for rb in range(ts // LRU_ROWS):
            lru_tasks.append(functools.partial(lru_block, q, rb))

    dt = dt_ref[0]
    d_a = dt * (-jnp.exp(alog_ref[...]))

    li = lax.broadcasted_iota(jnp.int32, (SSD_L, SSD_L), 0)
    si = lax.broadcasted_iota(jnp.int32, (SSD_L, SSD_L), 1)
    causal = li >= si
    tri = causal.astype(BF16)
    expand = (lax.broadcasted_iota(jnp.int32, (HP, SSD_INNER), 0)
              == (lax.broadcasted_iota(jnp.int32, (HP, SSD_INNER), 1) // SSD_P)).astype(BF16)

    def per_head_to_channels(v):
        return sum(_dot(p, expand) for p in _split_bf16(v, 2))

    chunk = {}

    def ssd_chunk_setup(c):
        cr = slice(c * SSD_L, (c + 1) * SSD_L)
        csum = sum(_dot(tri, p) for p in _split_bf16(d_a[cr], 3))
        last = csum[SSD_L - 1:SSD_L, :]
        csum2 = csum * LOG2E
        chunk[c] = dict(
            csum2=csum2, adj_t=(csum2 - jnp.log(dt[cr]) * LOG2E).T, ecs=jnp.exp(csum),
            w_x=_dot((dt[cr] * jnp.exp(last - csum)).astype(BF16), expand),
            el_x=per_head_to_channels(jnp.broadcast_to(jnp.exp(last), (SUBLANES, HP)))[0:1, :])

    unit = {}
    state = {}

    def ssd_stage_a(c, g):
        cr = slice(c * SSD_L, (c + 1) * SSD_L)
        b_g = xbc_ref[0, cr, SSD_INNER + g * SSD_N:SSD_INNER + (g + 1) * SSD_N]
        c_g = xbc_ref[0, cr, SSD_INNER + SSD_BC + g * SSD_N:SSD_INNER + SSD_BC + (g + 1) * SSD_N]
        b_t = b_g.astype(F32).T.astype(BF16)
        unit[c, g] = dict(b_t=b_t, cb=_dot(c_g, b_t))

    def ssd_stage_b(c, g):
        cr = slice(c * SSD_L, (c + 1) * SSD_L)
        k = chunk[c]
        c_f = xbc_ref[0, cr, SSD_INNER + SSD_BC + g * SSD_N:SSD_INNER + SSD_BC + (g + 1) * SSD_N].astype(F32)
        cb = unit[c, g].pop("cb")
        lhs = []
        for e in range(SSD_HPG):
            hh = g * SSD_HPG + e
            seg2 = k["csum2"][:, hh:hh + 1] - k["adj_t"][hh:hh + 1, :]
            m = (cb * jnp.exp2(jnp.where(causal, seg2, -jnp.inf))).astype(BF16)
            c2 = (c_f * k["ecs"][:, hh:hh + 1]).astype(BF16)
            lhs.append(jnp.concatenate([m, c2], axis=1))
        unit[c, g]["lhs"] = lhs

    def ssd_stage_c(c, g):
        cr = slice(c * SSD_L, (c + 1) * SSD_L)
        k = chunk[c]
        lhs = unit[c, g].pop("lhs")
        gs = slice(g * SSD_GW, (g + 1) * SSD_GW)
        s_g = state[g] if g in state else st_scr[g]
        s_b = s_g.astype(BF16)
        ys = []
        for e in range(SSD_HPG):
            hs = slice((g * SSD_HPG + e) * SSD_P, (g * SSD_HPG + e + 1) * SSD_P)
            rhs = jnp.concatenate([xbc_ref[0, cr, hs], s_b[:, e * SSD_P:(e + 1) * SSD_P]], axis=0)
            ys.append(_dot(lhs[e], rhs))
        xd_g = (xbc_ref[0, cr, gs].astype(F32) * k["w_x"][:, gs]).astype(BF16)
        state[g] = s_g * k["el_x"][:, gs] + _dot(unit.pop((c, g))["b_t"], xd_g)
        if c == ts // SSD_L - 1:
            st_scr[g] = state[g]
        yg = jnp.concatenate(ys, axis=1) + dexp_ref[:, gs] * xbc_ref[0, cr, gs].astype(F32)
        yg = yg * z_ref[0, cr, gs].astype(F32)
        yn_scr[cr, gs] = ((yg * _rms_scale(yg)) * nw_ref[:, gs]).astype(BF16)

    for c in range(ts // SSD_L):
        ssd_chunk_setup(c)
    units = [(c, g) for c in range(ts // SSD_L) for g in range(SSD_G)]
    ssd_tasks = []
    for n in range(len(units) + 2):
        for stage, lag in ((ssd_stage_a, 0), (ssd_stage_b, 1), (ssd_stage_c, 2)):
            if 0 <= n - lag < len(units):
                ssd_tasks.append(functools.partial(stage, *units[n - lag]))

    def lru_out():
        for q in range(n_q):
            hc_scr[:, q * qw:(q + 1) * qw] = carries[q]
        ffn["y_a"] = _dot(ya_scr[...], wpa_ref[...])

    lru_tasks.append(lru_out)

    lists = [(lru_tasks, 0.0, 0.8), (ssd_tasks, 0.0, 1.0), (ffn_tasks, 0.0, 1.0)]
    order = sorted((lo + (hi - lo) * (i + 0.5) / len(t), n, i)
                   for n, (t, lo, hi) in enumerate(lists) for i in range(len(t)))
    for _, n, i in order:
        lists[n][0][i]()

    y_a = ffn["y_a"]
    y_b = _dot(yn_scr[...], wpb_ref[...])
    merged = gt_ref[0, :, :D].astype(F32) * y_a + gt_ref[0, :, D:].astype(F32) * y_b
    y = _dot(merged.astype(BF16), wout_ref[...])
    x1_scr[cur] = x_ref[0] + g1_ref[0] * ((y * _rms_scale(y)) * post1_ref[...])


def _mixer_ffn(x, act, dt, g1, sc2, sh2, g2, post1, wg, ba, bx, lam, wpa, alog, dexp, nw, wpb, wout,
               pre2, post2, w1, w2, ts):
    bsz, seq, _ = x.shape
    nt = seq // ts
    n_tiles = bsz * nt

    def tile(width, col_off):
        def index(s):
            m = jnp.minimum(s, n_tiles - 1)
            return (m // nt, m % nt, col_off // width)
        return pl.BlockSpec((1, ts, width), index)

    def mixer_row(s):
        return (jnp.minimum(s, n_tiles - 1) // nt, 0, 0)

    def ffn_row(s):
        return (jnp.maximum(s - 1, 0) // nt, 0, 0)

    def ffn_tile(s):
        f = jnp.maximum(s - 1, 0)
        return (f // nt, f % nt, 0)

    in_specs = [
        tile(D, 0),
        tile(LRU_W, OFF_LX),
        tile(LRU_W, OFF_LG),
        tile(SSD_INNER, OFF_Z),
        tile(SSD_CONV, OFF_XBC),
        tile(2 * D, OFF_GT),
        tile(HP, 0),
        pl.BlockSpec((1, 1, D), mixer_row),
        pl.BlockSpec((1, 1, D), ffn_row),
        pl.BlockSpec((1, 1, D), ffn_row),
        pl.BlockSpec((1, 1, D), ffn_row),
    ]
    consts = [post1, wg, ba, bx, lam, wpa, alog, dexp, nw, wpb, wout, pre2, post2, w1, w2]
    in_specs += [pl.BlockSpec(a.shape, functools.partial(lambda nd, s: (0,) * nd, a.ndim),
                              pipeline_mode=pl.Buffered(1)) for a in consts]
    scratch = [
        pltpu.VMEM((LRU_W // (4 * LRU_HD), ts, 8 * LRU_HD), F32),
        pltpu.VMEM((ts, LRU_W), BF16),
        pltpu.VMEM((SUBLANES, LRU_W), F32),
        pltpu.VMEM((ts, SSD_INNER), BF16),
        pltpu.VMEM((SSD_G, SSD_N, SSD_GW), F32),
        pltpu.VMEM((2, ts, D), F32),
        pltpu.VMEM((ts, D_FF), BF16),
    ]
    return pl.pallas_call(
        functools.partial(_mixer_ffn_kernel, ts=ts, nt=nt),
        grid=(n_tiles + 1,),
        in_specs=in_specs,
        out_specs=pl.BlockSpec((1, ts, D), ffn_tile),
        out_shape=jax.ShapeDtypeStruct((bsz, seq, D), F32),
        scratch_shapes=scratch,
        compiler_params=pltpu.CompilerParams(
            dimension_semantics=("arbitrary",), vmem_limit_bytes=VMEM_LIMIT),
        name="mixer_ffn",
    )(x, act, act, act, act, act, dt, g1, sc2, sh2, g2, *consts)


def _block_diag4(w):
    w4 = w.reshape(LRU_HEADS // 4, 4, LRU_HD, LRU_HD)
    eye = jnp.eye(4, dtype=w.dtype)
    return jnp.einsum('qeij,ef->qeifj', w4, eye).reshape(LRU_HEADS // 4, 4 * LRU_HD, 4 * LRU_HD)


def _pad_heads(v, fill=0.0):
    return jnp.pad(v.reshape(1, SSD_H), ((0, 0), (0, HP - SSD_H)), constant_values=fill)


def _permute_w_in(w):
    o_lg, o_z, o_xbc = LRU_W, 2 * LRU_W, 2 * LRU_W + SSD_INNER
    o_dt = o_xbc + SSD_CONV
    o_gt = o_dt + SSD_H
    return jnp.concatenate(
        [w[:, o_xbc:o_dt], w[:, :o_lg], w[:, o_lg:o_z], w[:, o_z:o_xbc], w[:, o_gt:], w[:, o_dt:o_gt],
         jnp.zeros((D, HP - SSD_H), w.dtype)], axis=1).astype(BF16)


def kernel(x, c, w_ada, b_ada, pre_norm1, post_norm1, w_in, b_gate, lru_conv_w, lru_conv_b,
           lru_wa, lru_ba, lru_wx, lru_bx, lru_lambda, w_pa, ssd_conv_w, ssd_conv_b,
           ssd_dt_bias, ssd_a_log, ssd_d, ssd_norm_w, w_pb, w_out, pre_norm2, post_norm2,
           w_ff1, w_ff2):
    depth = w_ada.shape[0]
    bsz, seq, _ = x.shape
    ts = 256 if seq % 256 == 0 else SSD_L
    tm = 512 if seq % 512 == 0 else ts
    for l in range(depth):
        mod = _adaln(c, w_ada[l], b_ada[l]).reshape(bsz, N_MOD, 1, D)
        sh1, sc1, g1, sh2, sc2, g2 = (mod[:, k] for k in range(N_MOD))
        conv_w = jnp.concatenate([ssd_conv_w[l], lru_conv_w[l]], axis=1)
        conv_b = jnp.concatenate([ssd_conv_b[l], lru_conv_b[l]]).reshape(1, N_CONV)
        act, dt = _in_proj(x, sc1, sh1, pre_norm1[l].reshape(1, D), _permute_w_in(w_in[l]),
                           conv_w, conv_b, b_gate[l].reshape(1, 2 * D), _pad_heads(ssd_dt_bias[l]), tm)
        wg = jnp.concatenate([_block_diag4(lru_wa[l]), _block_diag4(lru_wx[l])], axis=2).astype(BF16)
        x = _mixer_ffn(
            x, act, dt, g1, sc2, sh2, g2, post_norm1[l].reshape(1, D), wg,
            lru_ba[l].reshape(1, LRU_W), lru_bx[l].reshape(1, LRU_W), lru_lambda[l].reshape(1, LRU_W),
            w_pa[l].astype(BF16), _pad_heads(ssd_a_log[l]),
            jnp.repeat(ssd_d[l], SSD_P).reshape(1, SSD_INNER), ssd_norm_w[l].reshape(1, SSD_INNER),
            w_pb[l].astype(BF16), w_out[l].astype(BF16),
            pre_norm2[l].reshape(1, D), post_norm2[l].reshape(1, D),
            w_ff1[l].astype(BF16), w_ff2[l].astype(BF16), ts)
    return x
```

```python
import functools
import math

import jax
import jax.numpy as jnp
from jax import lax
from jax.experimental import pallas as pl
from jax.experimental.pallas import tpu as pltpu

F32 = jnp.float32
BF16 = jnp.bfloat16

D = 1024
LRU_W = D
LRU_HEADS = 16
LRU_HD = LRU_W // LRU_HEADS
LRU_C = 8.0
CONV_K = 4
SSD_INNER = 2 * D
SSD_P = 64
SSD_H = SSD_INNER // SSD_P
SSD_G = 8
SSD_HPG = SSD_H // SSD_G
SSD_N = 128
SSD_L = 128
SSD_GW = SSD_HPG * SSD_P
SSD_BC = SSD_G * SSD_N
SSD_CONV = SSD_INNER + 2 * SSD_BC
D_FF = 4 * D
N_MOD = 6
EPS = 1e-6
LOG2E = math.log2(math.e)

LANES = 128
SUBLANES = 8
HP = LANES
OFF_XBC, OFF_LX = 0, SSD_CONV
N_CONV = SSD_CONV + LRU_W
OFF_LG = N_CONV
OFF_Z = OFF_LG + LRU_W
OFF_GT = OFF_Z + SSD_INNER
OFF_DT = OFF_GT + 2 * D
W_IN_P = OFF_DT + HP
W_ACT = OFF_DT
COL_CHUNK = 256
ROW_BLOCK = 64
LRU_ROWS = 64
FF_CHUNK = 1024

VMEM_LIMIT = 56 * 1024 * 1024


def _sigmoid(v):
    return 0.5 * jnp.tanh(0.5 * v) + 0.5


def _silu(v):
    h = 0.5 * v
    return h * jnp.tanh(h) + h


def _softplus(v):
    return jnp.maximum(v, 0.0) + jnp.log1p(jnp.exp(-jnp.abs(v)))


def _gelu_tanh(v):
    c = math.sqrt(2.0 / math.pi)
    h = 0.5 * v
    return h * jnp.tanh(v * (c + (c * 0.044715) * (v * v))) + h


def _rms_scale(v):
    return lax.rsqrt(jnp.mean(v * v, axis=-1, keepdims=True) + EPS)


def _split_bf16(v, n):
    parts = []
    r = v
    for _ in range(n):
        p = r.astype(BF16)
        parts.append(p)
        r = r - p.astype(F32)
    return parts


def _dot(a, b):
    return jnp.dot(a, b, preferred_element_type=F32)


def _adaln_kernel(c_ref, w_ref, b_ref, o_ref):
    c = c_ref[...]
    ca = (c * _sigmoid(c)).astype(BF16)
    o_ref[...] = _dot(ca, w_ref[...].astype(BF16)) + b_ref[...]


def _adaln(c, w_ada, b_ada):
    bsz = c.shape[0]
    n = w_ada.shape[1]
    tn = D
    return pl.pallas_call(
        _adaln_kernel,
        grid=(n // tn,),
        in_specs=[
            pl.BlockSpec((bsz, D), lambda j: (0, 0)),
            pl.BlockSpec((D, tn), lambda j: (0, j)),
            pl.BlockSpec((1, tn), lambda j: (0, j)),
        ],
        out_specs=pl.BlockSpec((bsz, tn), lambda j: (0, j)),
        out_shape=jax.ShapeDtypeStruct((bsz, n), F32),
        name="adaln_mod",
    )(c, w_ada, b_ada.reshape(1, n))


def _inproj_kernel(x_ref, sc_ref, sh_ref, nw_ref, w_ref, cw_ref, cb_ref, bgate_ref, dtb_ref,
                   o_ref, dt_ref, win0, win1, stg, tail, *, tm):
    @pl.when(pl.program_id(1) == 0)
    def _():
        tail[...] = jnp.zeros_like(tail)

    x = x_ref[0]
    h = (x * _rms_scale(x)) * nw_ref[...]
    h = h * (1.0 + sc_ref[0]) + sh_ref[0]
    hb = h.astype(BF16)
    n_slab = COL_CHUNK // LANES
    half = ROW_BLOCK // 2

    def project(j):
        c0 = j * COL_CHUNK
        pbuf = (win0, win1)[j % 2]
        p = _dot(hb, w_ref[:, c0:c0 + COL_CHUNK])
        for s in range(n_slab):
            ls = slice(c0 + s * LANES, c0 + (s + 1) * LANES)
            if c0 < N_CONV:
                pbuf[s, 0:SUBLANES, :] = tail[:, ls]
                tail[:, ls] = p[tm - SUBLANES:, s * LANES:(s + 1) * LANES]
            pbuf[s, SUBLANES:SUBLANES + tm, :] = p[:, s * LANES:(s + 1) * LANES]

    def epilogue(j):
        c0 = j * COL_CHUNK
        pbuf = (win0, win1)[j % 2]
        for s in range(n_slab):
            ls = slice(c0 + s * LANES, c0 + (s + 1) * LANES)
            for r0 in range(0, tm, ROW_BLOCK):
                if c0 < N_CONV:
                    taps = [pbuf[s, pl.ds(SUBLANES + r0 - 3 + m, half, stride=2), :] for m in range(5)]
                    outs = []
                    for par in range(2):
                        v = cb_ref[:, ls] + cw_ref[0:1, ls] * taps[par]
                        for k in range(1, CONV_K):
                            v = v + cw_ref[k:k + 1, ls] * taps[par + k]
                        if c0 < SSD_CONV:
                            v = _silu(v)
                        outs.append(v)
                    stg[s, pl.ds(r0, half, stride=2), :] = outs[0]
                    stg[s, pl.ds(r0 + 1, half, stride=2), :] = outs[1]
                    v = stg[s, r0:r0 + ROW_BLOCK, :]
                else:
                    v = pbuf[s, SUBLANES + r0:SUBLANES + r0 + ROW_BLOCK, :]
                    if c0 < OFF_Z:
                        v = _gelu_tanh(v)
                    elif c0 < OFF_GT:
                        v = _silu(v)
                    else:
                        v = _sigmoid(v + bgate_ref[:, c0 - OFF_GT + s * LANES:c0 - OFF_GT + (s + 1) * LANES])
                o_ref[0, r0:r0 + ROW_BLOCK, ls] = v.astype(BF16)

    n_chunks = W_ACT // COL_CHUNK
    project(0)
    for j in range(n_chunks):
        if j + 1 < n_chunks:
            project(j + 1)
        epilogue(j)
    dt_ref[0] = _softplus(_dot(hb, w_ref[:, OFF_DT:OFF_DT + HP]) + dtb_ref[...])


def _const_spec(shape):
    nd = len(shape)
    return pl.BlockSpec(shape, lambda b, i: (0,) * nd, pipeline_mode=pl.Buffered(1))


def _in_proj(x, sc, sh, nw, w_in_p, conv_w, conv_b, bgate, dtb, tm):
    bsz, seq, _ = x.shape
    row = pl.BlockSpec((1, 1, D), lambda b, i: (b, 0, 0))
    consts = [nw, w_in_p, conv_w, conv_b, bgate, dtb]
    return pl.pallas_call(
        functools.partial(_inproj_kernel, tm=tm),
        grid=(bsz, seq // tm),
        in_specs=[pl.BlockSpec((1, tm, D), lambda b, i: (b, i, 0)), row, row]
        + [_const_spec(a.shape) for a in consts],
        out_specs=[pl.BlockSpec((1, tm, W_ACT), lambda b, i: (b, i, 0)),
                   pl.BlockSpec((1, tm, HP), lambda b, i: (b, i, 0))],
        out_shape=[jax.ShapeDtypeStruct((bsz, seq, W_ACT), BF16),
                   jax.ShapeDtypeStruct((bsz, seq, HP), F32)],
        scratch_shapes=[
            pltpu.VMEM((COL_CHUNK // LANES, SUBLANES + tm, LANES), F32),
            pltpu.VMEM((COL_CHUNK // LANES, SUBLANES + tm, LANES), F32),
            pltpu.VMEM((COL_CHUNK // LANES, tm, LANES), F32),
            pltpu.VMEM((SUBLANES, N_CONV), F32)],
        compiler_params=pltpu.CompilerParams(
            dimension_semantics=("arbitrary", "arbitrary"), vmem_limit_bytes=VMEM_LIMIT),
        name="in_proj",
    )(x, sc, sh, *consts)


def _mixer_ffn_kernel(x_ref, lx_ref, lg_ref, z_ref, xbc_ref, gt_ref, dt_ref,
                      g1_ref, sc2_ref, sh2_ref, g2_ref,
                      post1_ref, wg_ref, ba_ref, bx_ref, lam_ref, wpa_ref,
                      alog_ref, dexp_ref, nw_ref, wpb_ref, wout_ref,
                      pre2_ref, post2_ref, w1_ref, w2_ref,
                      o_ref,
                      pre_scr, ya_scr, hc_scr, yn_scr, st_scr, x1_scr, act_scr,
                      *, ts, nt):
    s = pl.program_id(0)
    cur = s % 2

    @pl.when(s == 0)
    def _():
        x1_scr[...] = jnp.zeros_like(x1_scr)

    @pl.when(s % nt == 0)
    def _():
        hc_scr[...] = jnp.zeros_like(hc_scr)
        st_scr[...] = jnp.zeros_like(st_scr)

    def ffn_pre():
        xp = x1_scr[1 - cur]
        h = (xp * _rms_scale(xp)) * pre2_ref[...]
        ffn["hb"] = (h * (1.0 + sc2_ref[0]) + sh2_ref[0]).astype(BF16)

    def ffn_up(j):
        sl = slice(j * FF_CHUNK, (j + 1) * FF_CHUNK)
        a = jnp.maximum(_dot(ffn["hb"], w1_ref[:, sl]), 0.0)
        act_scr[:, sl] = (a * a).astype(BF16)

    def ffn_down(j):
        sl = slice(j * FF_CHUNK, (j + 1) * FF_CHUNK)
        part = _dot(act_scr[:, sl], w2_ref[sl, :])
        ffn["y"] = part if j == 0 else ffn["y"] + part

    def ffn_post():
        y = ffn["y"]
        o_ref[0] = x1_scr[1 - cur] + g2_ref[0] * ((y * _rms_scale(y)) * post2_ref[...])

    ffn = {}
    ffn_tasks = [ffn_pre]
    for j in range(D_FF // FF_CHUNK):
        ffn_tasks += [functools.partial(ffn_up, j), functools.partial(ffn_down, j)]
    ffn_tasks.append(ffn_post)

    sp = _softplus(-lam_ref[...])
    qw = 4 * LRU_HD
    n_q = LRU_W // qw
    row = lax.broadcasted_iota(jnp.int32, (SUBLANES, qw), 0)
    carries = [hc_scr[:, q * qw:(q + 1) * qw] for q in range(n_q)]

    def lru_gates(q):
        pre_scr[q] = _dot(lx_ref[0, :, q * qw:(q + 1) * qw], wg_ref[q])

    def lru_block(q, rb):
        cs = slice(q * qw, (q + 1) * qw)
        rs = slice(rb * LRU_ROWS, (rb + 1) * LRU_ROWS)
        r = _sigmoid(pre_scr[q, rs, :qw] + ba_ref[:, cs])
        ig = _sigmoid(pre_scr[q, rs, qw:] + bx_ref[:, cs])
        a = jnp.exp((-LRU_C * r) * sp[:, cs])
        u = jnp.sqrt(1.0 - a * a) * (ig * lx_ref[0, rs, cs].astype(F32))
        carry = carries[q]
        hs = []
        for k in range(LRU_ROWS // SUBLANES):
            av = a[k * SUBLANES:(k + 1) * SUBLANES]
            bv = u[k * SUBLANES:(k + 1) * SUBLANES]
            for s in (1, 2, 4):
                keep = row >= s
                a_sh = jnp.where(keep, pltpu.roll(av, s, 0), 1.0)
                b_sh = jnp.where(keep, pltpu.roll(bv, s, 0), 0.0)
                bv = av * b_sh + bv
                av = av * a_sh
            h = bv + av * carry
            hs.append(h)
            carry = jnp.broadcast_to(h[SUBLANES - 1:SUBLANES, :], (SUBLANES, qw))
        carries[q] = carry
        ya_scr[rs, cs] = (jnp.concatenate(hs, axis=0) * lg_ref[0, rs, cs].astype(F32)).astype(BF16)

    lru_tasks = []
    for q in range(n_q):
        lru_tasks.append(functools.partial(lru_gates, q))
        for rb in range(ts // LRU_ROWS):
            lru_tasks.append(functools.partial(lru_block, q, rb))

    dt = dt_ref[0]
    d_a = dt * (-jnp.exp(alog_ref[...]))

    li = lax.broadcasted_iota(jnp.int32, (SSD_L, SSD_L), 0)
    si = lax.broadcasted_iota(jnp.int32, (SSD_L, SSD_L), 1)
    causal = li >= si
    tri = causal.astype(BF16)
    expand = (lax.broadcasted_iota(jnp.int32, (HP, SSD_INNER), 0)
              == (lax.broadcasted_iota(jnp.int32, (HP, SSD_INNER), 1) // SSD_P)).astype(BF16)

    def per_head_to_channels(v):
        return sum(_dot(p, expand) for p in _split_bf16(v, 2))

    chunk = {}

    def ssd_chunk_setup(c):
        cr = slice(c * SSD_L, (c + 1) * SSD_L)
        csum = sum(_dot(tri, p) for p in _split_bf16(d_a[cr], 3))
        last = csum[SSD_L - 1:SSD_L, :]
        csum2 = csum * LOG2E
        chunk[c] = dict(
            csum2=csum2, adj_t=(csum2 - jnp.log(dt[cr]) * LOG2E).T, ecs=jnp.exp(csum),
            w_x=_dot((dt[cr] * jnp.exp(last - csum)).astype(BF16), expand),
            el_x=per_head_to_channels(jnp.broadcast_to(jnp.exp(last), (SUBLANES, HP)))[0:1, :])

    unit = {}
    state = {}

    def ssd_stage_a(c, g):
        cr = slice(c * SSD_L, (c + 1) * SSD_L)
        b_g = xbc_ref[0, cr, SSD_INNER + g * SSD_N:SSD_INNER + (g + 1) * SSD_N]
        c_g = xbc_ref[0, cr, SSD_INNER + SSD_BC + g * SSD_N:SSD_INNER + SSD_BC + (g + 1) * SSD_N]
        b_t = b_g.astype(F32).T.astype(BF16)
        unit[c, g] = dict(b_t=b_t, cb=_dot(c_g, b_t))

    def ssd_stage_b(c, g):
        cr = slice(c * SSD_L, (c + 1) * SSD_L)
        k = chunk[c]
        c_f = xbc_ref[0, cr, SSD_INNER + SSD_BC + g * SSD_N:SSD_INNER + SSD_BC + (g + 1) * SSD_N].astype(F32)
        cb = unit[c, g].pop("cb")
        lhs = []
        for e in range(SSD_HPG):
            hh = g * SSD_HPG + e
            seg2 = k["csum2"][:, hh:hh + 1] - k["adj_t"][hh:hh + 1, :]
            m = (cb * jnp.exp2(jnp.where(causal, seg2, -jnp.inf))).astype(BF16)
            c2 = (c_f * k["ecs"][:, hh:hh + 1]).astype(BF16)
            lhs.append(jnp.concatenate([m, c2], axis=1))
        unit[c, g]["lhs"] = lhs

    def ssd_stage_c(c, g):
        cr = slice(c * SSD_L, (c + 1) * SSD_L)
        k = chunk[c]
        lhs = unit[c, g].pop("lhs")
        gs = slice(g * SSD_GW, (g + 1) * SSD_GW)
        s_g = state[g] if g in state else st_scr[g]
        s_b = s_g.astype(BF16)
        ys = []
        for e in range(SSD_HPG):
            hs = slice((g * SSD_HPG + e) * SSD_P, (g * SSD_HPG + e + 1) * SSD_P)
            rhs = jnp.concatenate([xbc_ref[0, cr, hs], s_b[:, e * SSD_P:(e + 1) * SSD_P]], axis=0)
            ys.append(_dot(lhs[e], rhs))
        xd_g = (xbc_ref[0, cr, gs].astype(F32) * k["w_x"][:, gs]).astype(BF16)
        state[g] = s_g * k["el_x"][:, gs] + _dot(unit.pop((c, g))["b_t"], xd_g)
        if c == ts // SSD_L - 1:
            st_scr[g] = state[g]
        yg = jnp.concatenate(ys, axis=1) + dexp_ref[:, gs] * xbc_ref[0, cr, gs].astype(F32)
        yg = yg * z_ref[0, cr, gs].astype(F32)
        yn_scr[cr, gs] = ((yg * _rms_scale(yg)) * nw_ref[:, gs]).astype(BF16)

    for c in range(ts // SSD_L):
        ssd_chunk_setup(c)
    units = [(c, g) for c in range(ts // SSD_L) for g in range(SSD_G)]
    ssd_tasks = []
    for n in range(len(units) + 2):
        for stage, lag in ((ssd_stage_a, 0), (ssd_stage_b, 1), (ssd_stage_c, 2)):
            if 0 <= n - lag < len(units):
                ssd_tasks.append(functools.partial(stage, *units[n - lag]))

    def lru_out():
        for q in range(n_q):
            hc_scr[:, q * qw:(q + 1) * qw] = carries[q]
        ffn["y_a"] = _dot(ya_scr[...], wpa_ref[...])

    lru_tasks.append(lru_out)

    lists = [(lru_tasks, 0.0, 0.8), (ssd_tasks, 0.0, 1.0), (ffn_tasks, 0.0, 1.0)]
    order = sorted((lo + (hi - lo) * (i + 0.5) / len(t), n, i)
                   for n, (t, lo, hi) in enumerate(lists) for i in range(len(t)))
    for _, n, i in order:
        lists[n][0][i]()

    y_a = ffn["y_a"]
    y_b = _dot(yn_scr[...], wpb_ref[...])
    merged = gt_ref[0, :, :D].astype(F32) * y_a + gt_ref[0, :, D:].astype(F32) * y_b
    y = _dot(merged.astype(BF16), wout_ref[...])
    x1_scr[cur] = x_ref[0] + g1_ref[0] * ((y * _rms_scale(y)) * post1_ref[...])


def _mixer_ffn(x, act, dt, g1, sc2, sh2, g2, post1, wg, ba, bx, lam, wpa, alog, dexp, nw, wpb, wout,
               pre2, post2, w1, w2, ts):
    bsz, seq, _ = x.shape
    nt = seq // ts
    n_tiles = bsz * nt

    def tile(width, col_off):
        def index(s):
            m = jnp.minimum(s, n_tiles - 1)
            return (m // nt, m % nt, col_off // width)
        return pl.BlockSpec((1, ts, width), index)

    def mixer_row(s):
        return (jnp.minimum(s, n_tiles - 1) // nt, 0, 0)

    def ffn_row(s):
        return (jnp.maximum(s - 1, 0) // nt, 0, 0)

    def ffn_tile(s):
        f = jnp.maximum(s - 1, 0)
        return (f // nt, f % nt, 0)

    in_specs = [
        tile(D, 0),
        tile(LRU_W, OFF_LX),
        tile(LRU_W, OFF_LG),
        tile(SSD_INNER, OFF_Z),
        tile(SSD_CONV, OFF_XBC),
        tile(2 * D, OFF_GT),
        tile(HP, 0),
        pl.BlockSpec((1, 1, D), mixer_row),
        pl.BlockSpec((1, 1, D), ffn_row),
        pl.BlockSpec((1, 1, D), ffn_row),
        pl.BlockSpec((1, 1, D), ffn_row),
    ]
    consts = [post1, wg, ba, bx, lam, wpa, alog, dexp, nw, wpb, wout, pre2, post2, w1, w2]
    in_specs += [pl.BlockSpec(a.shape, functools.partial(lambda nd, s: (0,) * nd, a.ndim),
                              pipeline_mode=pl.Buffered(1)) for a in consts]
    scratch = [
        pltpu.VMEM((LRU_W // (4 * LRU_HD), ts, 8 * LRU_HD), F32),
        pltpu.VMEM((ts, LRU_W), BF16),
        pltpu.VMEM((SUBLANES, LRU_W), F32),
        pltpu.VMEM((ts, SSD_INNER), BF16),
        pltpu.VMEM((SSD_G, SSD_N, SSD_GW), F32),
        pltpu.VMEM((2, ts, D), F32),
        pltpu.VMEM((ts, D_FF), BF16),
    ]
    return pl.pallas_call(
        functools.partial(_mixer_ffn_kernel, ts=ts, nt=nt),
        grid=(n_tiles + 1,),
        in_specs=in_specs,
        out_specs=pl.BlockSpec((1, ts, D), ffn_tile),
        out_shape=jax.ShapeDtypeStruct((bsz, seq, D), F32),
        scratch_shapes=scratch,
        compiler_params=pltpu.CompilerParams(
            dimension_semantics=("arbitrary",), vmem_limit_bytes=VMEM_LIMIT),
        name="mixer_ffn",
    )(x, act, act, act, act, act, dt, g1, sc2, sh2, g2, *consts)


def _block_diag4(w):
    w4 = w.reshape(LRU_HEADS // 4, 4, LRU_HD, LRU_HD)
    eye = jnp.eye(4, dtype=w.dtype)
    return jnp.einsum('qeij,ef->qeifj', w4, eye).reshape(LRU_HEADS // 4, 4 * LRU_HD, 4 * LRU_HD)


def _pad_heads(v, fill=0.0):
    return jnp.pad(v.reshape(1, SSD_H), ((0, 0), (0, HP - SSD_H)), constant_values=fill)


def _permute_w_in(w):
    o_lg, o_z, o_xbc = LRU_W, 2 * LRU_W, 2 * LRU_W + SSD_INNER
    o_dt = o_xbc + SSD_CONV
    o_gt = o_dt + SSD_H
    return jnp.concatenate(
        [w[:, o_xbc:o_dt], w[:, :o_lg], w[:, o_lg:o_z], w[:, o_z:o_xbc], w[:, o_gt:], w[:, o_dt:o_gt],
         jnp.zeros((D, HP - SSD_H), w.dtype)], axis=1).astype(BF16)


def kernel(x, c, w_ada, b_ada, pre_norm1, post_norm1, w_in, b_gate, lru_conv_w, lru_conv_b,
           lru_wa, lru_ba, lru_wx, lru_bx, lru_lambda, w_pa, ssd_conv_w, ssd_conv_b,
           ssd_dt_bias, ssd_a_log, ssd_d, ssd_norm_w, w_pb, w_out, pre_norm2, post_norm2,
           w_ff1, w_ff2):
    depth = w_ada.shape[0]
    bsz, seq, _ = x.shape
    ts = 256 if seq % 256 == 0 else SSD_L
    tm = 512 if seq % 512 == 0 else ts
    for l in range(depth):
        mod = _adaln(c, w_ada[l], b_ada[l]).reshape(bsz, N_MOD, 1, D)
        sh1, sc1, g1, sh2, sc2, g2 = (mod[:, k] for k in range(N_MOD))
        conv_w = jnp.concatenate([ssd_conv_w[l], lru_conv_w[l]], axis=1)
        conv_b = jnp.concatenate([ssd_conv_b[l], lru_conv_b[l]]).reshape(1, N_CONV)
        act, dt = _in_proj(x, sc1, sh1, pre_norm1[l].reshape(1, D), _permute_w_in(w_in[l]),
                           conv_w, conv_b, b_gate[l].reshape(1, 2 * D), _pad_heads(ssd_dt_bias[l]), tm)
        wg = jnp.concatenate([_block_diag4(lru_wa[l]), _block_diag4(lru_wx[l])], axis=2).astype(BF16)
        x = _mixer_ffn(
            x, act, dt, g1, sc2, sh2, g2, post_norm1[l].reshape(1, D), wg,
            lru_ba[l].reshape(1, LRU_W), lru_bx[l].reshape(1, LRU_W), lru_lambda[l].reshape(1, LRU_W),
            w_pa[l].astype(BF16), _pad_heads(ssd_a_log[l]),
            jnp.repeat(ssd_d[l], SSD_P).reshape(1, SSD_INNER), ssd_norm_w[l].reshape(1, SSD_INNER),
            w_pb[l].astype(BF16), w_out[l].astype(BF16),
            pre_norm2[l].reshape(1, D), post_norm2[l].reshape(1, D),
            w_ff1[l].astype(BF16), w_ff2[l].astype(BF16), ts)
    return x
```

```python
import functools
import math

import jax
import jax.numpy as jnp
from jax import lax
from jax.experimental import pallas as pl
from jax.experimental.pallas import tpu as pltpu

F32 = jnp.float32
BF16 = jnp.bfloat16

D = 1024
LRU_W = D
LRU_HEADS = 16
LRU_HD = LRU_W // LRU_HEADS
LRU_C = 8.0
CONV_K = 4
SSD_INNER = 2 * D
SSD_P = 64
SSD_H = SSD_INNER // SSD_P
SSD_G = 8
SSD_HPG = SSD_H // SSD_G
SSD_N = 128
SSD_L = 128
SSD_GW = SSD_HPG * SSD_P
SSD_BC = SSD_G * SSD_N
SSD_CONV = SSD_INNER + 2 * SSD_BC
D_FF = 4 * D
N_MOD = 6
EPS = 1e-6
LOG2E = math.log2(math.e)

LANES = 128
SUBLANES = 8
HP = LANES
OFF_XBC, OFF_LX = 0, SSD_CONV
N_CONV = SSD_CONV + LRU_W
OFF_LG = N_CONV
OFF_Z = OFF_LG + LRU_W
OFF_GT = OFF_Z + SSD_INNER
OFF_DT = OFF_GT + 2 * D
W_IN_P = OFF_DT + HP
W_ACT = OFF_DT
COL_CHUNK = 256
ROW_BLOCK = 64
LRU_ROWS = 64
FF_CHUNK = 512

VMEM_LIMIT = 60 * 1024 * 1024


def _sigmoid(v):
    return 0.5 * jnp.tanh(0.5 * v) + 0.5


def _silu(v):
    h = 0.5 * v
    return h * jnp.tanh(h) + h


def _softplus(v):
    return jnp.maximum(v, 0.0) + jnp.log1p(jnp.exp(-jnp.abs(v)))


def _gelu_tanh(v):
    c = math.sqrt(2.0 / math.pi)
    h = 0.5 * v
    return h * jnp.tanh(v * (c + (c * 0.044715) * (v * v))) + h


def _rms_scale(v):
    return lax.rsqrt(jnp.mean(v * v, axis=-1, keepdims=True) + EPS)


def _split_bf16(v, n):
    parts = []
    r = v
    for _ in range(n):
        p = r.astype(BF16)
        parts.append(p)
        r = r - p.astype(F32)
    return parts


def _dot(a, b):
    return jnp.dot(a, b, preferred_element_type=F32)


def _adaln_kernel(c_ref, w_ref, b_ref, o_ref):
    c = c_ref[...]
    ca = (c * _sigmoid(c)).astype(BF16)
    o_ref[...] = _dot(ca, w_ref[...].astype(BF16)) + b_ref[...]


def _adaln(c, w_ada, b_ada):
    bsz = c.shape[0]
    n = w_ada.shape[1]
    tn = D
    return pl.pallas_call(
        _adaln_kernel,
        grid=(n // tn,),
        in_specs=[
            pl.BlockSpec((bsz, D), lambda j: (0, 0)),
            pl.BlockSpec((D, tn), lambda j: (0, j)),
            pl.BlockSpec((1, tn), lambda j: (0, j)),
        ],
        out_specs=pl.BlockSpec((bsz, tn), lambda j: (0, j)),
        out_shape=jax.ShapeDtypeStruct((bsz, n), F32),
        name="adaln_mod",
    )(c, w_ada, b_ada.reshape(1, n))


def _inproj_kernel(x_ref, sc_ref, sh_ref, nw_ref, w_ref, cw_ref, cb_ref, bgate_ref, dtb_ref,
                   o_ref, dt_ref, win0, win1, stg, tail, *, tm):
    @pl.when(pl.program_id(1) == 0)
    def _():
        tail[...] = jnp.zeros_like(tail)

    x = x_ref[0]
    h = (x * _rms_scale(x)) * nw_ref[...]
    h = h * (1.0 + sc_ref[0]) + sh_ref[0]
    hb = h.astype(BF16)
    n_slab = COL_CHUNK // LANES
    half = ROW_BLOCK // 2

    def project(j):
        c0 = j * COL_CHUNK
        pbuf = (win0, win1)[j % 2]
        p = _dot(hb, w_ref[:, c0:c0 + COL_CHUNK])
        for s in range(n_slab):
            ls = slice(c0 + s * LANES, c0 + (s + 1) * LANES)
            if c0 < N_CONV:
                pbuf[s, 0:SUBLANES, :] = tail[:, ls]
                tail[:, ls] = p[tm - SUBLANES:, s * LANES:(s + 1) * LANES]
            pbuf[s, SUBLANES:SUBLANES + tm, :] = p[:, s * LANES:(s + 1) * LANES]

    def epilogue(j):
        c0 = j * COL_CHUNK
        pbuf = (win0, win1)[j % 2]
        for s in range(n_slab):
            ls = slice(c0 + s * LANES, c0 + (s + 1) * LANES)
            for r0 in range(0, tm, ROW_BLOCK):
                if c0 < N_CONV:
                    taps = [pbuf[s, pl.ds(SUBLANES + r0 - 3 + m, half, stride=2), :] for m in range(5)]
                    outs = []
                    for par in range(2):
                        v = cb_ref[:, ls] + cw_ref[0:1, ls] * taps[par]
                        for k in range(1, CONV_K):
                            v = v + cw_ref[k:k + 1, ls] * taps[par + k]
                        if c0 < SSD_CONV:
                            v = _silu(v)
                        outs.append(v)
                    stg[s, pl.ds(r0, half, stride=2), :] = outs[0]
                    stg[s, pl.ds(r0 + 1, half, stride=2), :] = outs[1]
                    v = stg[s, r0:r0 + ROW_BLOCK, :]
                else:
                    v = pbuf[s, SUBLANES + r0:SUBLANES + r0 + ROW_BLOCK, :]
                    if c0 < OFF_Z:
                        v = _gelu_tanh(v)
                    elif c0 < OFF_GT:
                        v = _silu(v)
                    else:
                        v = _sigmoid(v + bgate_ref[:, c0 - OFF_GT + s * LANES:c0 - OFF_GT + (s + 1) * LANES])
                o_ref[0, r0:r0 + ROW_BLOCK, ls] = v.astype(BF16)

    n_chunks = W_ACT // COL_CHUNK
    project(0)
    for j in range(n_chunks):
        if j + 1 < n_chunks:
            project(j + 1)
        epilogue(j)
    dt_ref[0] = _softplus(_dot(hb, w_ref[:, OFF_DT:OFF_DT + HP]) + dtb_ref[...])


def _const_spec(shape):
    nd = len(shape)
    return pl.BlockSpec(shape, lambda b, i: (0,) * nd, pipeline_mode=pl.Buffered(1))


def _in_proj(x, sc, sh, nw, w_in_p, conv_w, conv_b, bgate, dtb, tm):
    bsz, seq, _ = x.shape
    row = pl.BlockSpec((1, 1, D), lambda b, i: (b, 0, 0))
    consts = [nw, w_in_p, conv_w, conv_b, bgate, dtb]
    return pl.pallas_call(
        functools.partial(_inproj_kernel, tm=tm),
        grid=(bsz, seq // tm),
        in_specs=[pl.BlockSpec((1, tm, D), lambda b, i: (b, i, 0)), row, row]
        + [_const_spec(a.shape) for a in consts],
        out_specs=[pl.BlockSpec((1, tm, W_ACT), lambda b, i: (b, i, 0)),
                   pl.BlockSpec((1, tm, HP), lambda b, i: (b, i, 0))],
        out_shape=[jax.ShapeDtypeStruct((bsz, seq, W_ACT), BF16),
                   jax.ShapeDtypeStruct((bsz, seq, HP), F32)],
        scratch_shapes=[
            pltpu.VMEM((COL_CHUNK // LANES, SUBLANES + tm, LANES), F32),
            pltpu.VMEM((COL_CHUNK // LANES, SUBLANES + tm, LANES), F32),
            pltpu.VMEM((COL_CHUNK // LANES, tm, LANES), F32),
            pltpu.VMEM((SUBLANES, N_CONV), F32)],
        compiler_params=pltpu.CompilerParams(
            dimension_semantics=("arbitrary", "arbitrary"), vmem_limit_bytes=VMEM_LIMIT),
        name="in_proj",
    )(x, sc, sh, *consts)


def _mixer_ffn_kernel(x_ref, lx_ref, lg_ref, z_ref, xbc_ref, gt_ref, dt_ref,
                      g1_ref, sc2_ref, sh2_ref, g2_ref,
                      post1_ref, wg_ref, ba_ref, bx_ref, lam_ref, wpa_ref,
                      alog_ref, dexp_ref, nw_ref, wpb_ref, wout_ref,
                      pre2_ref, post2_ref, w1_ref, w2_ref,
                      o_ref,
                      pre_scr, ya_scr, hc_scr, yn_scr, st_scr, x1_scr, act_scr,
                      *, ts, nt):
    s = pl.program_id(0)
    cur = s % 2

    @pl.when(s == 0)
    def _():
        x1_scr[...] = jnp.zeros_like(x1_scr)

    @pl.when(s % nt == 0)
    def _():
        hc_scr[...] = jnp.zeros_like(hc_scr)
        st_scr[...] = jnp.zeros_like(st_scr)

    def ffn_pre():
        xp = x1_scr[1 - cur]
        h = (xp * _rms_scale(xp)) * pre2_ref[...]
        ffn["hb"] = (h * (1.0 + sc2_ref[0]) + sh2_ref[0]).astype(BF16)

    def ffn_up(j):
        sl = slice(j * FF_CHUNK, (j + 1) * FF_CHUNK)
        a = jnp.maximum(_dot(ffn["hb"], w1_ref[:, sl]), 0.0)
        act_scr[:, sl] = (a * a).astype(BF16)

    def ffn_down(j):
        sl = slice(j * FF_CHUNK, (j + 1) * FF_CHUNK)
        part = _dot(act_scr[:, sl], w2_ref[sl, :])
        ffn["y"] = part if j == 0 else ffn["y"] + part

    def ffn_post():
        y = ffn["y"]
        o_ref[0] = x1_scr[1 - cur] + g2_ref[0] * ((y * _rms_scale(y)) * post2_ref[...])

    ffn = {}
    ffn_tasks = [ffn_pre]
    for j in range(D_FF // FF_CHUNK):
        ffn_tasks += [functools.partial(ffn_up, j), functools.partial(ffn_down, j)]
    ffn_tasks.append(ffn_post)

    k2 = (-0.5 * LRU_C * LOG2E) * _softplus(-lam_ref[...])
    qw = 4 * LRU_HD
    n_q = LRU_W // qw
    row = lax.broadcasted_iota(jnp.int32, (SUBLANES, qw), 0)
    carries = [hc_scr[:, q * qw:(q + 1) * qw] for q in range(n_q)]

    def lru_gates(q):
        pre_scr[q] = _dot(lx_ref[0, :, q * qw:(q + 1) * qw], wg_ref[q])

    def lru_block(q, rb):
        cs = slice(q * qw, (q + 1) * qw)
        rs = slice(rb * LRU_ROWS, (rb + 1) * LRU_ROWS)
        t_a = jnp.tanh(pre_scr[q, rs, :qw] + ba_ref[:, cs])
        t_x = jnp.tanh(pre_scr[q, rs, qw:] + bx_ref[:, cs])
        a = jnp.exp2(t_a * k2[:, cs] + k2[:, cs])
        xh = 0.5 * lx_ref[0, rs, cs].astype(F32)
        u = jnp.sqrt(1.0 - a * a) * (t_x * xh + xh)
        carry = carries[q]
        hs = []
        for k in range(LRU_ROWS // SUBLANES):
            av = a[k * SUBLANES:(k + 1) * SUBLANES]
            bv = u[k * SUBLANES:(k + 1) * SUBLANES]
            for s in (1, 2, 4):
                keep = row >= s
                a_sh = jnp.where(keep, pltpu.roll(av, s, 0), 1.0)
                b_sh = jnp.where(keep, pltpu.roll(bv, s, 0), 0.0)
                bv = av * b_sh + bv
                av = av * a_sh
            h = bv + av * carry
            hs.append(h)
            carry = jnp.broadcast_to(h[SUBLANES - 1:SUBLANES, :], (SUBLANES, qw))
        carries[q] = carry
        ya_scr[rs, cs] = (jnp.concatenate(hs, axis=0) * lg_ref[0, rs, cs].astype(F32)).astype(BF16)

    lru_tasks = []
    for q in range(n_q):
        lru_tasks.append(functools.partial(lru_gates, q))
        for rb in range(ts // LRU_ROWS):
            lru_tasks.append(functools.partial(lru_block, q, rb))

    dt = dt_ref[0]
    d_a = dt * (-jnp.exp(alog_ref[...]))

    li = lax.broadcasted_iota(jnp.int32, (SSD_L, SSD_L), 0)
    si = lax.broadcasted_iota(jnp.int32, (SSD_L, SSD_L), 1)
    causal = li >= si
    tri = causal.astype(BF16)
    expand = (lax.broadcasted_iota(jnp.int32, (HP, SSD_INNER), 0)
              == (lax.broadcasted_iota(jnp.int32, (HP, SSD_INNER), 1) // SSD_P)).astype(BF16)

    def per_head_to_channels(v):
        return sum(_dot(p, expand) for p in _split_bf16(v, 2))

    chunk = {}

    def ssd_chunk_setup(c):
        cr = slice(c * SSD_L, (c + 1) * SSD_L)
        csum = sum(_dot(tri, p) for p in _split_bf16(d_a[cr], 3))
        last = csum[SSD_L - 1:SSD_L, :]
        csum2 = csum * LOG2E
        chunk[c] = dict(
            csum2=csum2, adj_t=(csum2 - jnp.log(dt[cr]) * LOG2E).T, ecs=jnp.exp(csum),
            w_x=_dot((dt[cr] * jnp.exp(last - csum)).astype(BF16), expand),
            el_x=per_head_to_channels(jnp.broadcast_to(jnp.exp(last), (SUBLANES, HP)))[0:1, :])

    unit = {}
    state = {}

    def ssd_stage_a(c, g):
        cr = slice(c * SSD_L, (c + 1) * SSD_L)
        b_g = xbc_ref[0, cr, SSD_INNER + g * SSD_N:SSD_INNER + (g + 1) * SSD_N]
        c_g = xbc_ref[0, cr, SSD_INNER + SSD_BC + g * SSD_N:SSD_INNER + SSD_BC + (g + 1) * SSD_N]
        b_t = b_g.astype(F32).T.astype(BF16)
        unit[c, g] = dict(b_t=b_t, cb=_dot(c_g, b_t))

    def ssd_stage_b(c, g):
        cr = slice(c * SSD_L, (c + 1) * SSD_L)
        k = chunk[c]
        c_f = xbc_ref[0, cr, SSD_INNER + SSD_BC + g * SSD_N:SSD_INNER + SSD_BC + (g + 1) * SSD_N].astype(F32)
        cb = unit[c, g].pop("cb")
        lhs = []
        for e in range(SSD_HPG):
            hh = g * SSD_HPG + e
            seg2 = k["csum2"][:, hh:hh + 1] - k["adj_t"][hh:hh + 1, :]
            m = (cb * jnp.exp2(jnp.where(causal, seg2, -jnp.inf))).astype(BF16)
            c2 = (c_f * k["ecs"][:, hh:hh + 1]).astype(BF16)
            lhs.append(jnp.concatenate([m, c2], axis=1))
        unit[c, g]["lhs"] = lhs

    def ssd_stage_c(c, g):
        cr = slice(c * SSD_L, (c + 1) * SSD_L)
        k = chunk[c]
        lhs = unit[c, g].pop("lhs")
        gs = slice(g * SSD_GW, (g + 1) * SSD_GW)
        s_g = state[g] if g in state else st_scr[g]
        s_b = s_g.astype(BF16)
        ys = []
        for e in range(SSD_HPG):
            hs = slice((g * SSD_HPG + e) * SSD_P, (g * SSD_HPG + e + 1) * SSD_P)
            rhs = jnp.concatenate([xbc_ref[0, cr, hs], s_b[:, e * SSD_P:(e + 1) * SSD_P]], axis=0)
            ys.append(_dot(lhs[e], rhs))
        xd_g = (xbc_ref[0, cr, gs].astype(F32) * k["w_x"][:, gs]).astype(BF16)
        state[g] = s_g * k["el_x"][:, gs] + _dot(unit.pop((c, g))["b_t"], xd_g)
        if c == ts // SSD_L - 1:
            st_scr[g] = state[g]
        yg = jnp.concatenate(ys, axis=1) + dexp_ref[:, gs] * xbc_ref[0, cr, gs].astype(F32)
        yg = yg * z_ref[0, cr, gs].astype(F32)
        yn_scr[cr, gs] = ((yg * _rms_scale(yg)) * nw_ref[:, gs]).astype(BF16)

    for c in range(ts // SSD_L):
        ssd_chunk_setup(c)
    units = [(c, g) for c in range(ts // SSD_L) for g in range(SSD_G)]
    ssd_tasks = []
    for n in range(len(units) + 2):
        for stage, lag in ((ssd_stage_a, 0), (ssd_stage_b, 1), (ssd_stage_c, 2)):
            if 0 <= n - lag < len(units):
                ssd_tasks.append(functools.partial(stage, *units[n - lag]))

    def lru_out():
        for q in range(n_q):
            hc_scr[:, q * qw:(q + 1) * qw] = carries[q]
        ffn["y_a"] = _dot(ya_scr[...], wpa_ref[...])

    lru_tasks.append(lru_out)

    lists = [(lru_tasks, 0.0, 0.6), (ssd_tasks, 0.0, 1.0), (ffn_tasks, 0.0, 1.0)]
    order = sorted((lo + (hi - lo) * (i + 0.5) / len(t), n, i)
                   for n, (t, lo, hi) in enumerate(lists) for i in range(len(t)))
    for _, n, i in order:
        lists[n][0][i]()

    y_a = ffn["y_a"]
    y_b = _dot(yn_scr[...], wpb_ref[...])
    merged = gt_ref[0, :, :D].astype(F32) * y_a + gt_ref[0, :, D:].astype(F32) * y_b
    y = _dot(merged.astype(BF16), wout_ref[...])
    x1_scr[cur] = x_ref[0] + g1_ref[0] * ((y * _rms_scale(y)) * post1_ref[...])


def _mixer_ffn(x, act, dt, g1, sc2, sh2, g2, post1, wg, ba, bx, lam, wpa, alog, dexp, nw, wpb, wout,
               pre2, post2, w1, w2, ts):
    bsz, seq, _ = x.shape
    nt = seq // ts
    n_tiles = bsz * nt

    def tile(width, col_off):
        def index(s):
            m = jnp.minimum(s, n_tiles - 1)
            return (m // nt, m % nt, col_off // width)
        return pl.BlockSpec((1, ts, width), index)

    def mixer_row(s):
        return (jnp.minimum(s, n_tiles - 1) // nt, 0, 0)

    def ffn_row(s):
        return (jnp.maximum(s - 1, 0) // nt, 0, 0)

    def ffn_tile(s):
        f = jnp.maximum(s - 1, 0)
        return (f // nt, f % nt, 0)

    in_specs = [
        tile(D, 0),
        tile(LRU_W, OFF_LX),
        tile(LRU_W, OFF_LG),
        tile(SSD_INNER, OFF_Z),
        tile(SSD_CONV, OFF_XBC),
        tile(2 * D, OFF_GT),
        tile(HP, 0),
        pl.BlockSpec((1, 1, D), mixer_row),
        pl.BlockSpec((1, 1, D), ffn_row),
        pl.BlockSpec((1, 1, D), ffn_row),
        pl.BlockSpec((1, 1, D), ffn_row),
    ]
    consts = [post1, wg, ba, bx, lam, wpa, alog, dexp, nw, wpb, wout, pre2, post2, w1, w2]
    in_specs += [pl.BlockSpec(a.shape, functools.partial(lambda nd, s: (0,) * nd, a.ndim),
                              pipeline_mode=pl.Buffered(1)) for a in consts]
    scratch = [
        pltpu.VMEM((LRU_W // (4 * LRU_HD), ts, 8 * LRU_HD), F32),
        pltpu.VMEM((ts, LRU_W), BF16),
        pltpu.VMEM((SUBLANES, LRU_W), F32),
        pltpu.VMEM((ts, SSD_INNER), BF16),
        pltpu.VMEM((SSD_G, SSD_N, SSD_GW), F32),
        pltpu.VMEM((2, ts, D), F32),
        pltpu.VMEM((ts, D_FF), BF16),
    ]
    return pl.pallas_call(
        functools.partial(_mixer_ffn_kernel, ts=ts, nt=nt),
        grid=(n_tiles + 1,),
        in_specs=in_specs,
        out_specs=pl.BlockSpec((1, ts, D), ffn_tile),
        out_shape=jax.ShapeDtypeStruct((bsz, seq, D), F32),
        scratch_shapes=scratch,
        compiler_params=pltpu.CompilerParams(
            dimension_semantics=("arbitrary",), vmem_limit_bytes=VMEM_LIMIT),
        name="mixer_ffn",
    )(x, act, act, act, act, act, dt, g1, sc2, sh2, g2, *consts)


def _block_diag4(w):
    w4 = w.reshape(LRU_HEADS // 4, 4, LRU_HD, LRU_HD)
    eye = jnp.eye(4, dtype=w.dtype)
    return jnp.einsum('qeij,ef->qeifj', w4, eye).reshape(LRU_HEADS // 4, 4 * LRU_HD, 4 * LRU_HD)


def _pad_heads(v, fill=0.0):
    return jnp.pad(v.reshape(1, SSD_H), ((0, 0), (0, HP - SSD_H)), constant_values=fill)


def _permute_w_in(w):
    o_xbc = 2 * LRU_W + SSD_INNER
    o_dt = o_xbc + SSD_CONV
    o_gt = o_dt + SSD_H
    w = w.astype(BF16)
    return jnp.concatenate(
        [w[:, o_xbc:o_dt], w[:, :o_xbc], w[:, o_gt:], w[:, o_dt:o_gt], jnp.zeros((D, HP - SSD_H), BF16)], axis=1)


def kernel(x, c, w_ada, b_ada, pre_norm1, post_norm1, w_in, b_gate, lru_conv_w, lru_conv_b,
           lru_wa, lru_ba, lru_wx, lru_bx, lru_lambda, w_pa, ssd_conv_w, ssd_conv_b,
           ssd_dt_bias, ssd_a_log, ssd_d, ssd_norm_w, w_pb, w_out, pre_norm2, post_norm2,
           w_ff1, w_ff2):
    depth = w_ada.shape[0]
    bsz, seq, _ = x.shape
    ts = 256 if seq % 256 == 0 else SSD_L
    tm = 512 if seq % 512 == 0 else ts
    for l in range(depth):
        mod = _adaln(c, w_ada[l], b_ada[l]).reshape(bsz, N_MOD, 1, D)
        sh1, sc1, g1, sh2, sc2, g2 = (mod[:, k] for k in range(N_MOD))
        conv_w = jnp.concatenate([ssd_conv_w[l], lru_conv_w[l]], axis=1)
        conv_b = jnp.concatenate([ssd_conv_b[l], lru_conv_b[l]]).reshape(1, N_CONV)
        act, dt = _in_proj(x, sc1, sh1, pre_norm1[l].reshape(1, D), _permute_w_in(w_in[l]),
                           conv_w, conv_b, b_gate[l].reshape(1, 2 * D), _pad_heads(ssd_dt_bias[l]), tm)
        wg = (0.5 * jnp.concatenate([_block_diag4(lru_wa[l]), _block_diag4(lru_wx[l])], axis=2)).astype(BF16)
        x = _mixer_ffn(
            x, act, dt, g1, sc2, sh2, g2, post_norm1[l].reshape(1, D), wg,
            0.5 * lru_ba[l].reshape(1, LRU_W), 0.5 * lru_bx[l].reshape(1, LRU_W),
            lru_lambda[l].reshape(1, LRU_W),
            w_pa[l].astype(BF16), _pad_heads(ssd_a_log[l]),
            jnp.repeat(ssd_d[l], SSD_P).reshape(1, SSD_INNER), ssd_norm_w[l].reshape(1, SSD_INNER),
            w_pb[l].astype(BF16), w_out[l].astype(BF16),
            pre_norm2[l].reshape(1, D), post_norm2[l].reshape(1, D),
            w_ff1[l].astype(BF16), w_ff2[l].astype(BF16), ts)
    return x
```

```python
import functools
import math

import jax
import jax.numpy as jnp
from jax import lax
from jax.experimental import pallas as pl
from jax.experimental.pallas import tpu as pltpu

F32 = jnp.float32
BF16 = jnp.bfloat16

D = 1024
LRU_W = D
LRU_HEADS = 16
LRU_HD = LRU_W // LRU_HEADS
LRU_C = 8.0
CONV_K = 4
SSD_INNER = 2 * D
SSD_P = 64
SSD_H = SSD_INNER // SSD_P
SSD_G = 8
SSD_HPG = SSD_H // SSD_G
SSD_N = 128
SSD_L = 128
SSD_GW = SSD_HPG * SSD_P
SSD_BC = SSD_G * SSD_N
SSD_CONV = SSD_INNER + 2 * SSD_BC
D_FF = 4 * D
N_MOD = 6
EPS = 1e-6
LOG2E = math.log2(math.e)

LANES = 128
SUBLANES = 8
HP = LANES
OFF_XBC, OFF_LX = 0, SSD_CONV
N_CONV = SSD_CONV + LRU_W
OFF_LG = N_CONV
OFF_Z = OFF_LG + LRU_W
OFF_GT = OFF_Z + SSD_INNER
OFF_DT = OFF_GT + 2 * D
W_IN_P = OFF_DT + HP
W_ACT = OFF_DT
COL_CHUNK = 256
ROW_BLOCK = 64
LRU_ROWS = 64
FF_CHUNK = 512

VMEM_LIMIT = 60 * 1024 * 1024


def _sigmoid(v):
    return 0.5 * jnp.tanh(0.5 * v) + 0.5


def _silu(v):
    h = 0.5 * v
    return h * jnp.tanh(h) + h


def _softplus(v):
    return jnp.maximum(v, 0.0) + jnp.log1p(jnp.exp(-jnp.abs(v)))


def _gelu_tanh(v):
    c = math.sqrt(2.0 / math.pi)
    h = 0.5 * v
    return h * jnp.tanh(v * (c + (c * 0.044715) * (v * v))) + h


def _rms_scale(v):
    return lax.rsqrt(jnp.mean(v * v, axis=-1, keepdims=True) + EPS)


def _split_bf16(v, n):
    parts = []
    r = v
    for _ in range(n):
        p = r.astype(BF16)
        parts.append(p)
        r = r - p.astype(F32)
    return parts


def _dot(a, b):
    return jnp.dot(a, b, preferred_element_type=F32)


def _adaln_kernel(c_ref, w_ref, b_ref, o_ref):
    c = c_ref[...]
    ca = (c * _sigmoid(c)).astype(BF16)
    o_ref[...] = _dot(ca, w_ref[...].astype(BF16)) + b_ref[...]


def _adaln(c, w_ada, b_ada):
    bsz = c.shape[0]
    n = w_ada.shape[1]
    tn = D
    return pl.pallas_call(
        _adaln_kernel,
        grid=(n // tn,),
        in_specs=[
            pl.BlockSpec((bsz, D), lambda j: (0, 0)),
            pl.BlockSpec((D, tn), lambda j: (0, j)),
            pl.BlockSpec((1, tn), lambda j: (0, j)),
        ],
        out_specs=pl.BlockSpec((bsz, tn), lambda j: (0, j)),
        out_shape=jax.ShapeDtypeStruct((bsz, n), F32),
        name="adaln_mod",
    )(c, w_ada, b_ada.reshape(1, n))


def _inproj_kernel(x_ref, sc_ref, sh_ref, nw_ref, w_ref, cw_ref, cb_ref, bgate_ref, dtb_ref,
                   o_ref, dt_ref, win0, win1, stg, tail, *, tm):
    @pl.when(pl.program_id(1) == 0)
    def _():
        tail[...] = jnp.zeros_like(tail)

    x = x_ref[0]
    h = (x * _rms_scale(x)) * nw_ref[...]
    h = h * (1.0 + sc_ref[0]) + sh_ref[0]
    hb = h.astype(BF16)
    n_slab = COL_CHUNK // LANES
    half = ROW_BLOCK // 2

    def project(j):
        c0 = j * COL_CHUNK
        pbuf = (win0, win1)[j % 2]
        p = _dot(hb, w_ref[:, c0:c0 + COL_CHUNK])
        for s in range(n_slab):
            ls = slice(c0 + s * LANES, c0 + (s + 1) * LANES)
            if c0 < N_CONV:
                pbuf[s, 0:SUBLANES, :] = tail[:, ls]
                tail[:, ls] = p[tm - SUBLANES:, s * LANES:(s + 1) * LANES]
            pbuf[s, SUBLANES:SUBLANES + tm, :] = p[:, s * LANES:(s + 1) * LANES]

    def epilogue(j):
        c0 = j * COL_CHUNK
        pbuf = (win0, win1)[j % 2]
        for s in range(n_slab):
            ls = slice(c0 + s * LANES, c0 + (s + 1) * LANES)
            for r0 in range(0, tm, ROW_BLOCK):
                if c0 < N_CONV:
                    taps = [pbuf[s, pl.ds(SUBLANES + r0 - 3 + m, half, stride=2), :] for m in range(5)]
                    outs = []
                    for par in range(2):
                        v = cb_ref[:, ls] + cw_ref[0:1, ls] * taps[par]
                        for k in range(1, CONV_K):
                            v = v + cw_ref[k:k + 1, ls] * taps[par + k]
                        if c0 < SSD_CONV:
                            v = _silu(v)
                        outs.append(v)
                    stg[s, pl.ds(r0, half, stride=2), :] = outs[0]
                    stg[s, pl.ds(r0 + 1, half, stride=2), :] = outs[1]
                    v = stg[s, r0:r0 + ROW_BLOCK, :]
                else:
                    v = pbuf[s, SUBLANES + r0:SUBLANES + r0 + ROW_BLOCK, :]
                    if c0 < OFF_Z:
                        v = _gelu_tanh(v)
                    elif c0 < OFF_GT:
                        v = _silu(v)
                    else:
                        v = _sigmoid(v + bgate_ref[:, c0 - OFF_GT + s * LANES:c0 - OFF_GT + (s + 1) * LANES])
                o_ref[0, r0:r0 + ROW_BLOCK, ls] = v.astype(BF16)

    n_chunks = W_ACT // COL_CHUNK
    project(0)
    for j in range(n_chunks):
        if j + 1 < n_chunks:
            project(j + 1)
        epilogue(j)
    dt_ref[0] = _softplus(_dot(hb, w_ref[:, OFF_DT:OFF_DT + HP]) + dtb_ref[...])


def _const_spec(shape):
    nd = len(shape)
    return pl.BlockSpec(shape, lambda b, i: (0,) * nd, pipeline_mode=pl.Buffered(1))


def _in_proj(x, sc, sh, nw, w_in_p, conv_w, conv_b, bgate, dtb, tm):
    bsz, seq, _ = x.shape
    row = pl.BlockSpec((1, 1, D), lambda b, i: (b, 0, 0))
    consts = [nw, w_in_p, conv_w, conv_b, bgate, dtb]
    return pl.pallas_call(
        functools.partial(_inproj_kernel, tm=tm),
        grid=(bsz, seq // tm),
        in_specs=[pl.BlockSpec((1, tm, D), lambda b, i: (b, i, 0)), row, row]
        + [_const_spec(a.shape) for a in consts],
        out_specs=[pl.BlockSpec((1, tm, W_ACT), lambda b, i: (b, i, 0)),
                   pl.BlockSpec((1, tm, HP), lambda b, i: (b, i, 0))],
        out_shape=[jax.ShapeDtypeStruct((bsz, seq, W_ACT), BF16),
                   jax.ShapeDtypeStruct((bsz, seq, HP), F32)],
        scratch_shapes=[
            pltpu.VMEM((COL_CHUNK // LANES, SUBLANES + tm, LANES), F32),
            pltpu.VMEM((COL_CHUNK // LANES, SUBLANES + tm, LANES), F32),
            pltpu.VMEM((COL_CHUNK // LANES, tm, LANES), F32),
            pltpu.VMEM((SUBLANES, N_CONV), F32)],
        compiler_params=pltpu.CompilerParams(
            dimension_semantics=("arbitrary", "arbitrary"), vmem_limit_bytes=VMEM_LIMIT),
        name="in_proj",
    )(x, sc, sh, *consts)


def _mixer_ffn_kernel(x_ref, lx_ref, lg_ref, z_ref, xbc_ref, gt_ref, dt_ref,
                      g1_ref, sc2_ref, sh2_ref, g2_ref,
                      post1_ref, wg_ref, ba_ref, bx_ref, lam_ref, wpa_ref,
                      alog_ref, dexp_ref, nw_ref, wpb_ref, wout_ref,
                      pre2_ref, post2_ref, w1_ref, w2_ref,
                      o_ref,
                      pre_scr, ya_scr, hc_scr, yn_scr, st_scr, x1_scr, act_scr,
                      *, ts, nt):
    s = pl.program_id(0)
    cur = s % 2

    @pl.when(s == 0)
    def _():
        x1_scr[...] = jnp.zeros_like(x1_scr)

    @pl.when(s % nt == 0)
    def _():
        hc_scr[...] = jnp.zeros_like(hc_scr)
        st_scr[...] = jnp.zeros_like(st_scr)

    def ffn_pre():
        xp = x1_scr[1 - cur]
        h = (xp * _rms_scale(xp)) * pre2_ref[...]
        ffn["hb"] = (h * (1.0 + sc2_ref[0]) + sh2_ref[0]).astype(BF16)

    def ffn_up(j):
        sl = slice(j * FF_CHUNK, (j + 1) * FF_CHUNK)
        a = jnp.maximum(_dot(ffn["hb"], w1_ref[:, sl]), 0.0)
        act_scr[:, sl] = (a * a).astype(BF16)

    def ffn_down(j):
        sl = slice(j * FF_CHUNK, (j + 1) * FF_CHUNK)
        part = _dot(act_scr[:, sl], w2_ref[sl, :])
        ffn["y"] = part if j == 0 else ffn["y"] + part

    def ffn_post():
        y = ffn["y"]
        o_ref[0] = x1_scr[1 - cur] + g2_ref[0] * ((y * _rms_scale(y)) * post2_ref[...])

    ffn = {}
    ffn_tasks = [ffn_pre]
    for j in range(D_FF // FF_CHUNK):
        ffn_tasks += [functools.partial(ffn_up, j), functools.partial(ffn_down, j)]
    ffn_tasks.append(ffn_post)

    k2 = (-0.5 * LRU_C * LOG2E) * _softplus(-lam_ref[...])
    qw = 4 * LRU_HD
    n_q = LRU_W // qw
    row = lax.broadcasted_iota(jnp.int32, (SUBLANES, qw), 0)
    carries = [hc_scr[:, q * qw:(q + 1) * qw] for q in range(n_q)]

    def lru_gates(q):
        pre_scr[q] = _dot(lx_ref[0, :, q * qw:(q + 1) * qw], wg_ref[q])

    def lru_block(q, rb):
        cs = slice(q * qw, (q + 1) * qw)
        rs = slice(rb * LRU_ROWS, (rb + 1) * LRU_ROWS)
        t_a = jnp.tanh(pre_scr[q, rs, :qw] + ba_ref[:, cs])
        t_x = jnp.tanh(pre_scr[q, rs, qw:] + bx_ref[:, cs])
        a = jnp.exp2(t_a * k2[:, cs] + k2[:, cs])
        xh = 0.5 * lx_ref[0, rs, cs].astype(F32)
        u = jnp.sqrt(1.0 - a * a) * (t_x * xh + xh)
        carry = carries[q]
        hs = []
        for k in range(LRU_ROWS // SUBLANES):
            av = a[k * SUBLANES:(k + 1) * SUBLANES]
            bv = u[k * SUBLANES:(k + 1) * SUBLANES]
            for s in (1, 2, 4):
                keep = row >= s
                a_sh = jnp.where(keep, pltpu.roll(av, s, 0), 1.0)
                b_sh = jnp.where(keep, pltpu.roll(bv, s, 0), 0.0)
                bv = av * b_sh + bv
                av = av * a_sh
            h = bv + av * carry
            hs.append(h)
            carry = jnp.broadcast_to(h[SUBLANES - 1:SUBLANES, :], (SUBLANES, qw))
        carries[q] = carry
        ya_scr[rs, cs] = (jnp.concatenate(hs, axis=0) * lg_ref[0, rs, cs].astype(F32)).astype(BF16)

    lru_tasks = []
    for q in range(n_q):
        lru_tasks.append(functools.partial(lru_gates, q))
        for rb in range(ts // LRU_ROWS):
            lru_tasks.append(functools.partial(lru_block, q, rb))

    dt = dt_ref[0]
    d_a = dt * (-jnp.exp(alog_ref[...]))

    li = lax.broadcasted_iota(jnp.int32, (SSD_L, SSD_L), 0)
    si = lax.broadcasted_iota(jnp.int32, (SSD_L, SSD_L), 1)
    causal = li >= si
    tri = causal.astype(BF16)
    expand = (lax.broadcasted_iota(jnp.int32, (HP, SSD_INNER), 0)
              == (lax.broadcasted_iota(jnp.int32, (HP, SSD_INNER), 1) // SSD_P)).astype(BF16)

    def per_head_to_channels(v):
        return sum(_dot(p, expand) for p in _split_bf16(v, 2))

    chunk = {}

    def ssd_chunk_setup(c):
        cr = slice(c * SSD_L, (c + 1) * SSD_L)
        csum = sum(_dot(tri, p) for p in _split_bf16(d_a[cr], 3))
        last = csum[SSD_L - 1:SSD_L, :]
        csum2 = csum * LOG2E
        chunk[c] = dict(
            csum2=csum2, adj_t=(csum2 - jnp.log(dt[cr]) * LOG2E).T, ecs=jnp.exp(csum),
            w_x=_dot((dt[cr] * jnp.exp(last - csum)).astype(BF16), expand),
            el_x=per_head_to_channels(jnp.broadcast_to(jnp.exp(last), (SUBLANES, HP)))[0:1, :])

    unit = {}
    state = {}

    def ssd_stage_a(c, g):
        cr = slice(c * SSD_L, (c + 1) * SSD_L)
        b_g = xbc_ref[0, cr, SSD_INNER + g * SSD_N:SSD_INNER + (g + 1) * SSD_N]
        c_g = xbc_ref[0, cr, SSD_INNER + SSD_BC + g * SSD_N:SSD_INNER + SSD_BC + (g + 1) * SSD_N]
        b_t = b_g.astype(F32).T.astype(BF16)
        unit[c, g] = dict(b_t=b_t, cb=_dot(c_g, b_t))

    def ssd_stage_b(c, g):
        cr = slice(c * SSD_L, (c + 1) * SSD_L)
        k = chunk[c]
        c_f = xbc_ref[0, cr, SSD_INNER + SSD_BC + g * SSD_N:SSD_INNER + SSD_BC + (g + 1) * SSD_N].astype(F32)
        cb = unit[c, g].pop("cb")
        lhs = []
        for e in range(SSD_HPG):
            hh = g * SSD_HPG + e
            seg2 = k["csum2"][:, hh:hh + 1] - k["adj_t"][hh:hh + 1, :]
            m = (cb * jnp.exp2(jnp.where(causal, seg2, -jnp.inf))).astype(BF16)
            c2 = (c_f * k["ecs"][:, hh:hh + 1]).astype(BF16)
            lhs.append(jnp.concatenate([m, c2], axis=1))
        unit[c, g]["lhs"] = lhs

    def ssd_stage_c(c, g):
        cr = slice(c * SSD_L, (c + 1) * SSD_L)
        k = chunk[c]
        lhs = unit[c, g].pop("lhs")
        gs = slice(g * SSD_GW, (g + 1) * SSD_GW)
        s_g = state[g] if g in state else st_scr[g]
        s_b = s_g.astype(BF16)
        ys = []
        for e in range(SSD_HPG):
            hs = slice((g * SSD_HPG + e) * SSD_P, (g * SSD_HPG + e + 1) * SSD_P)
            rhs = jnp.concatenate([xbc_ref[0, cr, hs], s_b[:, e * SSD_P:(e + 1) * SSD_P]], axis=0)
            ys.append(_dot(lhs[e], rhs))
        xd_g = (xbc_ref[0, cr, gs].astype(F32) * k["w_x"][:, gs]).astype(BF16)
        state[g] = s_g * k["el_x"][:, gs] + _dot(unit.pop((c, g))["b_t"], xd_g)
        if c == ts // SSD_L - 1:
            st_scr[g] = state[g]
        yg = jnp.concatenate(ys, axis=1) + dexp_ref[:, gs] * xbc_ref[0, cr, gs].astype(F32)
        yg = yg * z_ref[0, cr, gs].astype(F32)
        yn_scr[cr, gs] = ((yg * _rms_scale(yg)) * nw_ref[:, gs]).astype(BF16)

    for c in range(ts // SSD_L):
        ssd_chunk_setup(c)
    units = [(c, g) for c in range(ts // SSD_L) for g in range(SSD_G)]
    ssd_tasks = []
    for n in range(len(units) + 2):
        for stage, lag in ((ssd_stage_a, 0), (ssd_stage_b, 1), (ssd_stage_c, 2)):
            if 0 <= n - lag < len(units):
                ssd_tasks.append(functools.partial(stage, *units[n - lag]))

    def lru_out():
        for q in range(n_q):
            hc_scr[:, q * qw:(q + 1) * qw] = carries[q]
        ffn["y_a"] = _dot(ya_scr[...], wpa_ref[...])

    lru_tasks.append(lru_out)

    lists = [(lru_tasks, 0.0, 0.6), (ssd_tasks, 0.0, 1.0), (ffn_tasks, 0.0, 1.0)]
    order = sorted((lo + (hi - lo) * (i + 0.5) / len(t), n, i)
                   for n, (t, lo, hi) in enumerate(lists) for i in range(len(t)))
    for _, n, i in order:
        lists[n][0][i]()

    y_a = ffn["y_a"]
    y_b = _dot(yn_scr[...], wpb_ref[...])
    merged = gt_ref[0, :, :D].astype(F32) * y_a + gt_ref[0, :, D:].astype(F32) * y_b
    y = _dot(merged.astype(BF16), wout_ref[...])
    x1_scr[cur] = x_ref[0] + g1_ref[0] * ((y * _rms_scale(y)) * post1_ref[...])


def _mixer_ffn(x, act, dt, g1, sc2, sh2, g2, post1, wg, ba, bx, lam, wpa, alog, dexp, nw, wpb, wout,
               pre2, post2, w1, w2, ts):
    bsz, seq, _ = x.shape
    nt = seq // ts
    n_tiles = bsz * nt

    def tile(width, col_off):
        def index(s):
            m = jnp.minimum(s, n_tiles - 1)
            return (m // nt, m % nt, col_off // width)
        return pl.BlockSpec((1, ts, width), index)

    def mixer_row(s):
        return (jnp.minimum(s, n_tiles - 1) // nt, 0, 0)

    def ffn_row(s):
        return (jnp.maximum(s - 1, 0) // nt, 0, 0)

    def ffn_tile(s):
        f = jnp.maximum(s - 1, 0)
        return (f // nt, f % nt, 0)

    in_specs = [
        tile(D, 0),
        tile(LRU_W, OFF_LX),
        tile(LRU_W, OFF_LG),
        tile(SSD_INNER, OFF_Z),
        tile(SSD_CONV, OFF_XBC),
        tile(2 * D, OFF_GT),
        tile(HP, 0),
        pl.BlockSpec((1, 1, D), mixer_row),
        pl.BlockSpec((1, 1, D), ffn_row),
        pl.BlockSpec((1, 1, D), ffn_row),
        pl.BlockSpec((1, 1, D), ffn_row),
    ]
    consts = [post1, wg, ba, bx, lam, wpa, alog, dexp, nw, wpb, wout, pre2, post2, w1, w2]
    in_specs += [pl.BlockSpec(a.shape, functools.partial(lambda nd, s: (0,) * nd, a.ndim),
                              pipeline_mode=pl.Buffered(1)) for a in consts]
    scratch = [
        pltpu.VMEM((LRU_W // (4 * LRU_HD), ts, 8 * LRU_HD), F32),
        pltpu.VMEM((ts, LRU_W), BF16),
        pltpu.VMEM((SUBLANES, LRU_W), F32),
        pltpu.VMEM((ts, SSD_INNER), BF16),
        pltpu.VMEM((SSD_G, SSD_N, SSD_GW), F32),
        pltpu.VMEM((2, ts, D), F32),
        pltpu.VMEM((ts, D_FF), BF16),
    ]
    return pl.pallas_call(
        functools.partial(_mixer_ffn_kernel, ts=ts, nt=nt),
        grid=(n_tiles + 1,),
        in_specs=in_specs,
        out_specs=pl.BlockSpec((1, ts, D), ffn_tile),
        out_shape=jax.ShapeDtypeStruct((bsz, seq, D), F32),
        scratch_shapes=scratch,
        compiler_params=pltpu.CompilerParams(
            dimension_semantics=("arbitrary",), vmem_limit_bytes=VMEM_LIMIT),
        name="mixer_ffn",
    )(x, act, act, act, act, act, dt, g1, sc2, sh2, g2, *consts)


def _block_diag4(w):
    w4 = w.reshape(LRU_HEADS // 4, 4, LRU_HD, LRU_HD)
    eye = jnp.eye(4, dtype=w.dtype)
    return jnp.einsum('qeij,ef->qeifj', w4, eye).reshape(LRU_HEADS // 4, 4 * LRU_HD, 4 * LRU_HD)


def _pad_heads(v, fill=0.0):
    return jnp.pad(v.reshape(1, SSD_H), ((0, 0), (0, HP - SSD_H)), constant_values=fill)


def _permute_w_in(w):
    o_xbc = 2 * LRU_W + SSD_INNER
    o_dt = o_xbc + SSD_CONV
    o_gt = o_dt + SSD_H
    w = w.astype(BF16)
    return jnp.concatenate(
        [w[:, o_xbc:o_dt], w[:, :o_xbc], w[:, o_gt:], w[:, o_dt:o_gt], jnp.zeros((D, HP - SSD_H), BF16)], axis=1)


def kernel(x, c, w_ada, b_ada, pre_norm1, post_norm1, w_in, b_gate, lru_conv_w, lru_conv_b,
           lru_wa, lru_ba, lru_wx, lru_bx, lru_lambda, w_pa, ssd_conv_w, ssd_conv_b,
           ssd_dt_bias, ssd_a_log, ssd_d, ssd_norm_w, w_pb, w_out, pre_norm2, post_norm2,
           w_ff1, w_ff2):
    depth = w_ada.shape[0]
    bsz, seq, _ = x.shape
    ts = 256 if seq % 256 == 0 else SSD_L
    tm = ts
    for l in range(depth):
        mod = _adaln(c, w_ada[l], b_ada[l]).reshape(bsz, N_MOD, 1, D)
        sh1, sc1, g1, sh2, sc2, g2 = (mod[:, k] for k in range(N_MOD))
        conv_w = jnp.concatenate([ssd_conv_w[l], lru_conv_w[l]], axis=1)
        conv_b = jnp.concatenate([ssd_conv_b[l], lru_conv_b[l]]).reshape(1, N_CONV)
        act, dt = _in_proj(x, sc1, sh1, pre_norm1[l].reshape(1, D), _permute_w_in(w_in[l]),
                           conv_w, conv_b, b_gate[l].reshape(1, 2 * D), _pad_heads(ssd_dt_bias[l]), tm)
        wg = (0.5 * jnp.concatenate([_block_diag4(lru_wa[l]), _block_diag4(lru_wx[l])], axis=2)).astype(BF16)
        x = _mixer_ffn(
            x, act, dt, g1, sc2, sh2, g2, post_norm1[l].reshape(1, D), wg,
            0.5 * lru_ba[l].reshape(1, LRU_W), 0.5 * lru_bx[l].reshape(1, LRU_W),
            lru_lambda[l].reshape(1, LRU_W),
            w_pa[l].astype(BF16), _pad_heads(ssd_a_log[l]),
            jnp.repeat(ssd_d[l], SSD_P).reshape(1, SSD_INNER), ssd_norm_w[l].reshape(1, SSD_INNER),
            w_pb[l].astype(BF16), w_out[l].astype(BF16),
            pre_norm2[l].reshape(1, D), post_norm2[l].reshape(1, D),
            w_ff1[l].astype(BF16), w_ff2[l].astype(BF16), ts)
    return x
```

```python
import functools
import math

import jax
import jax.numpy as jnp
from jax import lax
from jax.experimental import pallas as pl
from jax.experimental.pallas import tpu as pltpu

F32 = jnp.float32
BF16 = jnp.bfloat16

D = 1024
LRU_W = D
LRU_HEADS = 16
LRU_HD = LRU_W // LRU_HEADS
LRU_C = 8.0
CONV_K = 4
SSD_INNER = 2 * D
SSD_P = 64
SSD_H = SSD_INNER // SSD_P
SSD_G = 8
SSD_HPG = SSD_H // SSD_G
SSD_N = 128
SSD_L = 128
SSD_GW = SSD_HPG * SSD_P
SSD_BC = SSD_G * SSD_N
SSD_CONV = SSD_INNER + 2 * SSD_BC
D_FF = 4 * D
N_MOD = 6
EPS = 1e-6
LOG2E = math.log2(math.e)

LANES = 128
SUBLANES = 8
HP = LANES
OFF_XBC, OFF_LX = 0, SSD_CONV
N_CONV = SSD_CONV + LRU_W
OFF_LG = N_CONV
OFF_Z = OFF_LG + LRU_W
OFF_GT = OFF_Z + SSD_INNER
OFF_DT = OFF_GT + 2 * D
W_IN_P = OFF_DT + HP
W_ACT = OFF_DT
COL_CHUNK = 256
ROW_BLOCK = 64
LRU_ROWS = 64
FF_CHUNK = 512

VMEM_LIMIT = 60 * 1024 * 1024


def _sigmoid(v):
    return 0.5 * jnp.tanh(0.5 * v) + 0.5


def _silu(v):
    h = 0.5 * v
    return h * jnp.tanh(h) + h


def _softplus(v):
    return jnp.maximum(v, 0.0) + jnp.log1p(jnp.exp(-jnp.abs(v)))


def _gelu_tanh(v):
    c = math.sqrt(2.0 / math.pi)
    h = 0.5 * v
    return h * jnp.tanh(v * (c + (c * 0.044715) * (v * v))) + h


def _rms_scale(v):
    return lax.rsqrt(jnp.mean(v * v, axis=-1, keepdims=True) + EPS)


def _split_bf16(v, n):
    parts = []
    r = v
    for _ in range(n):
        p = r.astype(BF16)
        parts.append(p)
        r = r - p.astype(F32)
    return parts


def _dot(a, b):
    return jnp.dot(a, b, preferred_element_type=F32)


def _adaln_kernel(c_ref, w_ref, b_ref, o_ref):
    c = c_ref[...]
    ca = (c * _sigmoid(c)).astype(BF16)
    o_ref[...] = _dot(ca, w_ref[...].astype(BF16)) + b_ref[...]


def _adaln(c, w_ada, b_ada):
    bsz = c.shape[0]
    n = w_ada.shape[1]
    tn = D
    return pl.pallas_call(
        _adaln_kernel,
        grid=(n // tn,),
        in_specs=[
            pl.BlockSpec((bsz, D), lambda j: (0, 0)),
            pl.BlockSpec((D, tn), lambda j: (0, j)),
            pl.BlockSpec((1, tn), lambda j: (0, j)),
        ],
        out_specs=pl.BlockSpec((bsz, tn), lambda j: (0, j)),
        out_shape=jax.ShapeDtypeStruct((bsz, n), F32),
        name="adaln_mod",
    )(c, w_ada, b_ada.reshape(1, n))


def _inproj_kernel(x_ref, sc_ref, sh_ref, nw_ref, w_ref, cw_ref, cb_ref, bgate_ref, dtb_ref,
                   o_ref, dt_ref, win0, win1, stg, tail, *, tm):
    @pl.when(pl.program_id(1) == 0)
    def _():
        tail[...] = jnp.zeros_like(tail)

    x = x_ref[0]
    h = (x * _rms_scale(x)) * nw_ref[...]
    h = h * (1.0 + sc_ref[0]) + sh_ref[0]
    hb = h.astype(BF16)
    n_slab = COL_CHUNK // LANES
    half = ROW_BLOCK // 2

    def project(j):
        c0 = j * COL_CHUNK
        pbuf = (win0, win1)[j % 2]
        p = _dot(hb, w_ref[:, c0:c0 + COL_CHUNK])
        for s in range(n_slab):
            ls = slice(c0 + s * LANES, c0 + (s + 1) * LANES)
            if c0 < N_CONV:
                pbuf[s, 0:SUBLANES, :] = tail[:, ls]
                tail[:, ls] = p[tm - SUBLANES:, s * LANES:(s + 1) * LANES]
            pbuf[s, SUBLANES:SUBLANES + tm, :] = p[:, s * LANES:(s + 1) * LANES]

    def epilogue(j):
        c0 = j * COL_CHUNK
        pbuf = (win0, win1)[j % 2]
        for s in range(n_slab):
            ls = slice(c0 + s * LANES, c0 + (s + 1) * LANES)
            for r0 in range(0, tm, ROW_BLOCK):
                if c0 < N_CONV:
                    taps = [pbuf[s, pl.ds(SUBLANES + r0 - 3 + m, half, stride=2), :] for m in range(5)]
                    outs = []
                    for par in range(2):
                        v = cb_ref[:, ls] + cw_ref[0:1, ls] * taps[par]
                        for k in range(1, CONV_K):
                            v = v + cw_ref[k:k + 1, ls] * taps[par + k]
                        if c0 < SSD_CONV:
                            v = _silu(v)
                        outs.append(v)
                    stg[s, pl.ds(r0, half, stride=2), :] = outs[0]
                    stg[s, pl.ds(r0 + 1, half, stride=2), :] = outs[1]
                    v = stg[s, r0:r0 + ROW_BLOCK, :]
                else:
                    v = pbuf[s, SUBLANES + r0:SUBLANES + r0 + ROW_BLOCK, :]
                    if c0 < OFF_Z:
                        v = _gelu_tanh(v)
                    elif c0 < OFF_GT:
                        v = _silu(v)
                    else:
                        v = _sigmoid(v + bgate_ref[:, c0 - OFF_GT + s * LANES:c0 - OFF_GT + (s + 1) * LANES])
                o_ref[0, r0:r0 + ROW_BLOCK, ls] = v.astype(BF16)

    n_chunks = W_ACT // COL_CHUNK
    project(0)
    for j in range(n_chunks):
        if j + 1 < n_chunks:
            project(j + 1)
        epilogue(j)
    dt_ref[0] = _softplus(_dot(hb, w_ref[:, OFF_DT:OFF_DT + HP]) + dtb_ref[...])


def _const_spec(shape):
    nd = len(shape)
    return pl.BlockSpec(shape, lambda b, i: (0,) * nd, pipeline_mode=pl.Buffered(1))


def _in_proj(x, sc, sh, nw, w_in_p, conv_w, conv_b, bgate, dtb, tm):
    bsz, seq, _ = x.shape
    row = pl.BlockSpec((1, 1, D), lambda b, i: (b, 0, 0))
    consts = [nw, w_in_p, conv_w, conv_b, bgate, dtb]
    return pl.pallas_call(
        functools.partial(_inproj_kernel, tm=tm),
        grid=(bsz, seq // tm),
        in_specs=[pl.BlockSpec((1, tm, D), lambda b, i: (b, i, 0)), row, row]
        + [_const_spec(a.shape) for a in consts],
        out_specs=[pl.BlockSpec((1, tm, W_ACT), lambda b, i: (b, i, 0)),
                   pl.BlockSpec((1, tm, HP), lambda b, i: (b, i, 0))],
        out_shape=[jax.ShapeDtypeStruct((bsz, seq, W_ACT), BF16),
                   jax.ShapeDtypeStruct((bsz, seq, HP), F32)],
        scratch_shapes=[
            pltpu.VMEM((COL_CHUNK // LANES, SUBLANES + tm, LANES), F32),
            pltpu.VMEM((COL_CHUNK // LANES, SUBLANES + tm, LANES), F32),
            pltpu.VMEM((COL_CHUNK // LANES, tm, LANES), F32),
            pltpu.VMEM((SUBLANES, N_CONV), F32)],
        compiler_params=pltpu.CompilerParams(
            dimension_semantics=("arbitrary", "arbitrary"), vmem_limit_bytes=VMEM_LIMIT),
        name="in_proj",
    )(x, sc, sh, *consts)


def _mixer_ffn_kernel(x_ref, lx_ref, lg_ref, z_ref, xbc_ref, gt_ref, dt_ref,
                      g1_ref, sc2_ref, sh2_ref, g2_ref,
                      post1_ref, wg_ref, ba_ref, bx_ref, lam_ref, wpa_ref,
                      alog_ref, dexp_ref, nw_ref, wpb_ref, wout_ref,
                      pre2_ref, post2_ref, w1_ref, w2_ref,
                      o_ref,
                      pre_scr, ya_scr, hc_scr, yn_scr, st_scr, x1_scr, act_scr,
                      *, ts, nt):
    s = pl.program_id(0)
    cur = s % 2

    @pl.when(s == 0)
    def _():
        x1_scr[...] = jnp.zeros_like(x1_scr)

    @pl.when(s % nt == 0)
    def _():
        hc_scr[...] = jnp.zeros_like(hc_scr)
        st_scr[...] = jnp.zeros_like(st_scr)

    def ffn_pre():
        xp = x1_scr[1 - cur]
        h = (xp * _rms_scale(xp)) * pre2_ref[...]
        ffn["hb"] = (h * (1.0 + sc2_ref[0]) + sh2_ref[0]).astype(BF16)

    def ffn_up(j):
        sl = slice(j * FF_CHUNK, (j + 1) * FF_CHUNK)
        a = jnp.maximum(_dot(ffn["hb"], w1_ref[:, sl]), 0.0)
        act_scr[:, sl] = (a * a).astype(BF16)

    def ffn_down(j):
        sl = slice(j * FF_CHUNK, (j + 1) * FF_CHUNK)
        part = _dot(act_scr[:, sl], w2_ref[sl, :])
        ffn["y"] = part if j == 0 else ffn["y"] + part

    def ffn_post():
        y = ffn["y"]
        o_ref[0] = x1_scr[1 - cur] + g2_ref[0] * ((y * _rms_scale(y)) * post2_ref[...])

    ffn = {}
    ffn_tasks = [ffn_pre]
    for j in range(D_FF // FF_CHUNK):
        ffn_tasks += [functools.partial(ffn_up, j), functools.partial(ffn_down, j)]
    ffn_tasks.append(ffn_post)

    k2 = (-0.5 * LRU_C * LOG2E) * _softplus(-lam_ref[...])
    qw = 4 * LRU_HD
    n_q = LRU_W // qw
    row = lax.broadcasted_iota(jnp.int32, (SUBLANES, qw), 0)
    carries = [hc_scr[:, q * qw:(q + 1) * qw] for q in range(n_q)]

    def lru_gates(q):
        pre_scr[q] = _dot(lx_ref[0, :, q * qw:(q + 1) * qw], wg_ref[q])

    def lru_block(q, rb):
        cs = slice(q * qw, (q + 1) * qw)
        rs = slice(rb * LRU_ROWS, (rb + 1) * LRU_ROWS)
        t_a = jnp.tanh(pre_scr[q, rs, :qw] + ba_ref[:, cs])
        t_x = jnp.tanh(pre_scr[q, rs, qw:] + bx_ref[:, cs])
        a = jnp.exp2(t_a * k2[:, cs] + k2[:, cs])
        xh = 0.5 * lx_ref[0, rs, cs].astype(F32)
        u = jnp.sqrt(1.0 - a * a) * (t_x * xh + xh)
        carry = carries[q]
        hs = []
        for k in range(LRU_ROWS // SUBLANES):
            av = a[k * SUBLANES:(k + 1) * SUBLANES]
            bv = u[k * SUBLANES:(k + 1) * SUBLANES]
            for s in (1, 2, 4):
                keep = row >= s
                a_sh = jnp.where(keep, pltpu.roll(av, s, 0), 1.0)
                b_sh = jnp.where(keep, pltpu.roll(bv, s, 0), 0.0)
                bv = av * b_sh + bv
                av = av * a_sh
            h = bv + av * carry
            hs.append(h)
            carry = jnp.broadcast_to(h[SUBLANES - 1:SUBLANES, :], (SUBLANES, qw))
        carries[q] = carry
        ya_scr[rs, cs] = (jnp.concatenate(hs, axis=0) * lg_ref[0, rs, cs].astype(F32)).astype(BF16)

    lru_tasks = []
    for q in range(n_q):
        lru_tasks.append(functools.partial(lru_gates, q))
        for rb in range(ts // LRU_ROWS):
            lru_tasks.append(functools.partial(lru_block, q, rb))

    dt = dt_ref[0]
    d_a = dt * (-jnp.exp(alog_ref[...]))

    li = lax.broadcasted_iota(jnp.int32, (SSD_L, SSD_L), 0)
    si = lax.broadcasted_iota(jnp.int32, (SSD_L, SSD_L), 1)
    tri = (li >= si).astype(BF16)

    chunk = {}

    def ssd_chunk_setup(c):
        cr = slice(c * SSD_L, (c + 1) * SSD_L)
        csum = sum(_dot(tri, p) for p in _split_bf16(d_a[cr], 3))
        last = csum[SSD_L - 1:SSD_L, :]
        csum2 = csum * LOG2E
        ecs_t = jnp.exp(csum).T
        chunk[c] = dict(
            csum2_t=csum2.T, adj=csum2 - jnp.log(dt[cr]) * LOG2E, ecs_t=ecs_t,
            w_t=(dt[cr] * jnp.exp(last - csum)).T)

    unit = {}
    state = {}
    causal_t = li <= si

    def ssd_stage_a(c, g):
        cr = slice(c * SSD_L, (c + 1) * SSD_L)
        b_g = xbc_ref[0, cr, SSD_INNER + g * SSD_N:SSD_INNER + (g + 1) * SSD_N]
        c_g = xbc_ref[0, cr, SSD_INNER + SSD_BC + g * SSD_N:SSD_INNER + SSD_BC + (g + 1) * SSD_N]
        cb_t = lax.dot_general(b_g, c_g, (((1,), (1,)), ((), ())), preferred_element_type=F32)
        unit[c, g] = dict(cb_t=cb_t, c_t=c_g.astype(F32).T)

    def ssd_stage_b(c, g):
        k = chunk[c]
        u = unit[c, g]
        cb_t, c_t = u.pop("cb_t"), u.pop("c_t")
        rhs = []
        for e in range(SSD_HPG):
            hh = g * SSD_HPG + e
            seg2 = k["csum2_t"][hh:hh + 1, :] - k["adj"][:, hh:hh + 1]
            m_t = (cb_t * jnp.exp2(jnp.where(causal_t, seg2, -jnp.inf))).astype(BF16)
            c2_t = (c_t * k["ecs_t"][hh:hh + 1, :]).astype(BF16)
            rhs.append(jnp.concatenate([m_t, c2_t], axis=0))
        u["rhs"] = rhs

    def ssd_stage_c(c, g):
        cr = slice(c * SSD_L, (c + 1) * SSD_L)
        k = chunk[c]
        rhs = unit.pop((c, g))["rhs"]
        gs = slice(g * SSD_GW, (g + 1) * SSD_GW)
        xs_g = xbc_ref[0, cr, gs].astype(F32)
        x_t = xs_g.T
        s_t = state[g] if g in state else st_scr[g]
        ys_t, xd_t, sdec = [], [], []
        for e in range(SSD_HPG):
            hh = g * SSD_HPG + e
            rows = slice(e * SSD_P, (e + 1) * SSD_P)
            lhs = jnp.concatenate([x_t[rows].astype(BF16), s_t[rows].astype(BF16)], axis=1)
            ys_t.append(_dot(lhs, rhs[e]))
            xd_t.append((x_t[rows] * k["w_t"][hh:hh + 1, :]).astype(BF16))
            sdec.append(s_t[rows] * k["ecs_t"][hh:hh + 1, SSD_L - 1:SSD_L])
        b_g = xbc_ref[0, cr, SSD_INNER + g * SSD_N:SSD_INNER + (g + 1) * SSD_N]
        state[g] = jnp.concatenate(sdec, axis=0) + _dot(jnp.concatenate(xd_t, axis=0), b_g)
        if c == ts // SSD_L - 1:
            st_scr[g] = state[g]
        yg = jnp.concatenate(ys_t, axis=0).T + dexp_ref[:, gs] * xs_g
        yg = yg * z_ref[0, cr, gs].astype(F32)
        yn_scr[cr, gs] = ((yg * _rms_scale(yg)) * nw_ref[:, gs]).astype(BF16)

    for c in range(ts // SSD_L):
        ssd_chunk_setup(c)
    units = [(c, g) for c in range(ts // SSD_L) for g in range(SSD_G)]
    ssd_tasks = []
    for n in range(len(units) + 2):
        for stage, lag in ((ssd_stage_a, 0), (ssd_stage_b, 1), (ssd_stage_c, 2)):
            if 0 <= n - lag < len(units):
                ssd_tasks.append(functools.partial(stage, *units[n - lag]))

    def lru_out():
        for q in range(n_q):
            hc_scr[:, q * qw:(q + 1) * qw] = carries[q]
        ffn["y_a"] = _dot(ya_scr[...], wpa_ref[...])

    lru_tasks.append(lru_out)

    lists = [(lru_tasks, 0.0, 0.6), (ssd_tasks, 0.0, 1.0), (ffn_tasks, 0.0, 1.0)]
    order = sorted((lo + (hi - lo) * (i + 0.5) / len(t), n, i)
                   for n, (t, lo, hi) in enumerate(lists) for i in range(len(t)))
    for _, n, i in order:
        lists[n][0][i]()

    y_a = ffn["y_a"]
    y_b = _dot(yn_scr[...], wpb_ref[...])
    merged = gt_ref[0, :, :D].astype(F32) * y_a + gt_ref[0, :, D:].astype(F32) * y_b
    y = _dot(merged.astype(BF16), wout_ref[...])
    x1_scr[cur] = x_ref[0] + g1_ref[0] * ((y * _rms_scale(y)) * post1_ref[...])


def _mixer_ffn(x, act, dt, g1, sc2, sh2, g2, post1, wg, ba, bx, lam, wpa, alog, dexp, nw, wpb, wout,
               pre2, post2, w1, w2, ts):
    bsz, seq, _ = x.shape
    nt = seq // ts
    n_tiles = bsz * nt

    def tile(width, col_off):
        def index(s):
            m = jnp.minimum(s, n_tiles - 1)
            return (m // nt, m % nt, col_off // width)
        return pl.BlockSpec((1, ts, width), index)

    def mixer_row(s):
        return (jnp.minimum(s, n_tiles - 1) // nt, 0, 0)

    def ffn_row(s):
        return (jnp.maximum(s - 1, 0) // nt, 0, 0)

    def ffn_tile(s):
        f = jnp.maximum(s - 1, 0)
        return (f // nt, f % nt, 0)

    in_specs = [
        tile(D, 0),
        tile(LRU_W, OFF_LX),
        tile(LRU_W, OFF_LG),
        tile(SSD_INNER, OFF_Z),
        tile(SSD_CONV, OFF_XBC),
        tile(2 * D, OFF_GT),
        tile(HP, 0),
        pl.BlockSpec((1, 1, D), mixer_row),
        pl.BlockSpec((1, 1, D), ffn_row),
        pl.BlockSpec((1, 1, D), ffn_row),
        pl.BlockSpec((1, 1, D), ffn_row),
    ]
    consts = [post1, wg, ba, bx, lam, wpa, alog, dexp, nw, wpb, wout, pre2, post2, w1, w2]
    in_specs += [pl.BlockSpec(a.shape, functools.partial(lambda nd, s: (0,) * nd, a.ndim),
                              pipeline_mode=pl.Buffered(1)) for a in consts]
    scratch = [
        pltpu.VMEM((LRU_W // (4 * LRU_HD), ts, 8 * LRU_HD), F32),
        pltpu.VMEM((ts, LRU_W), BF16),
        pltpu.VMEM((SUBLANES, LRU_W), F32),
        pltpu.VMEM((ts, SSD_INNER), BF16),
        pltpu.VMEM((SSD_G, SSD_GW, SSD_N), F32),
        pltpu.VMEM((2, ts, D), F32),
        pltpu.VMEM((ts, D_FF), BF16),
    ]
    return pl.pallas_call(
        functools.partial(_mixer_ffn_kernel, ts=ts, nt=nt),
        grid=(n_tiles + 1,),
        in_specs=in_specs,
        out_specs=pl.BlockSpec((1, ts, D), ffn_tile),
        out_shape=jax.ShapeDtypeStruct((bsz, seq, D), F32),
        scratch_shapes=scratch,
        compiler_params=pltpu.CompilerParams(
            dimension_semantics=("arbitrary",), vmem_limit_bytes=VMEM_LIMIT),
        name="mixer_ffn",
    )(x, act, act, act, act, act, dt, g1, sc2, sh2, g2, *consts)


def _block_diag4(w):
    w4 = w.reshape(LRU_HEADS // 4, 4, LRU_HD, LRU_HD)
    eye = jnp.eye(4, dtype=w.dtype)
    return jnp.einsum('qeij,ef->qeifj', w4, eye).reshape(LRU_HEADS // 4, 4 * LRU_HD, 4 * LRU_HD)


def _pad_heads(v, fill=0.0):
    return jnp.pad(v.reshape(1, SSD_H), ((0, 0), (0, HP - SSD_H)), constant_values=fill)


def _permute_w_in(w):
    o_xbc = 2 * LRU_W + SSD_INNER
    o_dt = o_xbc + SSD_CONV
    o_gt = o_dt + SSD_H
    w = w.astype(BF16)
    return jnp.concatenate(
        [w[:, o_xbc:o_dt], w[:, :o_xbc], w[:, o_gt:], w[:, o_dt:o_gt], jnp.zeros((D, HP - SSD_H), BF16)], axis=1)


def kernel(x, c, w_ada, b_ada, pre_norm1, post_norm1, w_in, b_gate, lru_conv_w, lru_conv_b,
           lru_wa, lru_ba, lru_wx, lru_bx, lru_lambda, w_pa, ssd_conv_w, ssd_conv_b,
           ssd_dt_bias, ssd_a_log, ssd_d, ssd_norm_w, w_pb, w_out, pre_norm2, post_norm2,
           w_ff1, w_ff2):
    depth = w_ada.shape[0]
    bsz, seq, _ = x.shape
    ts = 256 if seq % 256 == 0 else SSD_L
    tm = 512 if seq % 512 == 0 else ts
    for l in range(depth):
        mod = _adaln(c, w_ada[l], b_ada[l]).reshape(bsz, N_MOD, 1, D)
        sh1, sc1, g1, sh2, sc2, g2 = (mod[:, k] for k in range(N_MOD))
        conv_w = jnp.concatenate([ssd_conv_w[l], lru_conv_w[l]], axis=1)
        conv_b = jnp.concatenate([ssd_conv_b[l], lru_conv_b[l]]).reshape(1, N_CONV)
        act, dt = _in_proj(x, sc1, sh1, pre_norm1[l].reshape(1, D), _permute_w_in(w_in[l]),
                           conv_w, conv_b, b_gate[l].reshape(1, 2 * D), _pad_heads(ssd_dt_bias[l]), tm)
        wg = (0.5 * jnp.concatenate([_block_diag4(lru_wa[l]), _block_diag4(lru_wx[l])], axis=2)).astype(BF16)
        x = _mixer_ffn(
            x, act, dt, g1, sc2, sh2, g2, post_norm1[l].reshape(1, D), wg,
            0.5 * lru_ba[l].reshape(1, LRU_W), 0.5 * lru_bx[l].reshape(1, LRU_W),
            lru_lambda[l].reshape(1, LRU_W),
            w_pa[l].astype(BF16), _pad_heads(ssd_a_log[l]),
            jnp.repeat(ssd_d[l], SSD_P).reshape(1, SSD_INNER), ssd_norm_w[l].reshape(1, SSD_INNER),
            w_pb[l].astype(BF16), w_out[l].astype(BF16),
            pre_norm2[l].reshape(1, D), post_norm2[l].reshape(1, D),
            w_ff1[l].astype(BF16), w_ff2[l].astype(BF16), ts)
    return x
```

```python
import functools
import math

import jax
import jax.numpy as jnp
from jax import lax
from jax.experimental import pallas as pl
from jax.experimental.pallas import tpu as pltpu

F32 = jnp.float32
BF16 = jnp.bfloat16

D = 1024
LRU_W = D
LRU_HEADS = 16
LRU_HD = LRU_W // LRU_HEADS
LRU_C = 8.0
CONV_K = 4
SSD_INNER = 2 * D
SSD_P = 64
SSD_H = SSD_INNER // SSD_P
SSD_G = 8
SSD_HPG = SSD_H // SSD_G
SSD_N = 128
SSD_L = 128
SSD_GW = SSD_HPG * SSD_P
SSD_BC = SSD_G * SSD_N
SSD_CONV = SSD_INNER + 2 * SSD_BC
D_FF = 4 * D
N_MOD = 6
EPS = 1e-6
LOG2E = math.log2(math.e)

LANES = 128
SUBLANES = 8
HP = LANES
OFF_XBC, OFF_LX = 0, SSD_CONV
N_CONV = SSD_CONV + LRU_W
OFF_LG = N_CONV
OFF_Z = OFF_LG + LRU_W
OFF_GT = OFF_Z + SSD_INNER
OFF_DT = OFF_GT + 2 * D
W_IN_P = OFF_DT + HP
W_ACT = OFF_DT
COL_CHUNK = 256
ROW_BLOCK = 64
LRU_ROWS = 64
FF_CHUNK = 512

VMEM_LIMIT = 60 * 1024 * 1024


def _sigmoid(v):
    return 0.5 * jnp.tanh(0.5 * v) + 0.5


def _silu(v):
    h = 0.5 * v
    return h * jnp.tanh(h) + h


def _softplus(v):
    return jnp.maximum(v, 0.0) + jnp.log1p(jnp.exp(-jnp.abs(v)))


def _gelu_tanh(v):
    c = math.sqrt(2.0 / math.pi)
    h = 0.5 * v
    return h * jnp.tanh(v * (c + (c * 0.044715) * (v * v))) + h


def _rms_scale(v):
    return lax.rsqrt(jnp.mean(v * v, axis=-1, keepdims=True) + EPS)


def _split_bf16(v, n):
    parts = []
    r = v
    for _ in range(n):
        p = r.astype(BF16)
        parts.append(p)
        r = r - p.astype(F32)
    return parts


def _dot(a, b):
    return jnp.dot(a, b, preferred_element_type=F32)


def _adaln_kernel(c_ref, w_ref, b_ref, o_ref):
    c = c_ref[...]
    ca = (c * _sigmoid(c)).astype(BF16)
    o_ref[...] = _dot(ca, w_ref[...].astype(BF16)) + b_ref[...]


def _adaln(c, w_ada, b_ada):
    bsz = c.shape[0]
    n = w_ada.shape[1]
    tn = D
    return pl.pallas_call(
        _adaln_kernel,
        grid=(n // tn,),
        in_specs=[
            pl.BlockSpec((bsz, D), lambda j: (0, 0)),
            pl.BlockSpec((D, tn), lambda j: (0, j)),
            pl.BlockSpec((1, tn), lambda j: (0, j)),
        ],
        out_specs=pl.BlockSpec((bsz, tn), lambda j: (0, j)),
        out_shape=jax.ShapeDtypeStruct((bsz, n), F32),
        name="adaln_mod",
    )(c, w_ada, b_ada.reshape(1, n))


def _inproj_kernel(x_ref, sc_ref, sh_ref, nw_ref, w_ref, cw_ref, cb_ref, bgate_ref, dtb_ref,
                   o_ref, dt_ref, win0, win1, stg, tail, *, tm):
    @pl.when(pl.program_id(1) == 0)
    def _():
        tail[...] = jnp.zeros_like(tail)

    x = x_ref[0]
    h = (x * _rms_scale(x)) * nw_ref[...]
    h = h * (1.0 + sc_ref[0]) + sh_ref[0]
    hb = h.astype(BF16)
    n_slab = COL_CHUNK // LANES
    half = ROW_BLOCK // 2

    def project(j):
        c0 = j * COL_CHUNK
        pbuf = (win0, win1)[j % 2]
        p = _dot(hb, w_ref[:, c0:c0 + COL_CHUNK])
        for s in range(n_slab):
            ls = slice(c0 + s * LANES, c0 + (s + 1) * LANES)
            if c0 < N_CONV:
                pbuf[s, 0:SUBLANES, :] = tail[:, ls]
                tail[:, ls] = p[tm - SUBLANES:, s * LANES:(s + 1) * LANES]
            pbuf[s, SUBLANES:SUBLANES + tm, :] = p[:, s * LANES:(s + 1) * LANES]

    def epilogue(j):
        c0 = j * COL_CHUNK
        pbuf = (win0, win1)[j % 2]
        for s in range(n_slab):
            ls = slice(c0 + s * LANES, c0 + (s + 1) * LANES)
            for r0 in range(0, tm, ROW_BLOCK):
                if c0 < N_CONV:
                    taps = [pbuf[s, pl.ds(SUBLANES + r0 - 3 + m, half, stride=2), :] for m in range(5)]
                    outs = []
                    for par in range(2):
                        v = cb_ref[:, ls] + cw_ref[0:1, ls] * taps[par]
                        for k in range(1, CONV_K):
                            v = v + cw_ref[k:k + 1, ls] * taps[par + k]
                        if c0 < SSD_CONV:
                            v = _silu(v)
                        outs.append(v)
                    stg[s, pl.ds(r0, half, stride=2), :] = outs[0]
                    stg[s, pl.ds(r0 + 1, half, stride=2), :] = outs[1]
                    v = stg[s, r0:r0 + ROW_BLOCK, :]
                else:
                    v = pbuf[s, SUBLANES + r0:SUBLANES + r0 + ROW_BLOCK, :]
                    if c0 < OFF_Z:
                        v = _gelu_tanh(v)
                    elif c0 < OFF_GT:
                        v = _silu(v)
                    else:
                        v = _sigmoid(v + bgate_ref[:, c0 - OFF_GT + s * LANES:c0 - OFF_GT + (s + 1) * LANES])
                o_ref[0, r0:r0 + ROW_BLOCK, ls] = v.astype(BF16)

    n_chunks = W_ACT // COL_CHUNK
    project(0)
    for j in range(n_chunks):
        if j + 1 < n_chunks:
            project(j + 1)
        epilogue(j)
    dt_ref[0] = _softplus(_dot(hb, w_ref[:, OFF_DT:OFF_DT + HP]) + dtb_ref[...])


def _const_spec(shape):
    nd = len(shape)
    return pl.BlockSpec(shape, lambda b, i: (0,) * nd, pipeline_mode=pl.Buffered(1))


def _in_proj(x, sc, sh, nw, w_in_p, conv_w, conv_b, bgate, dtb, tm):
    bsz, seq, _ = x.shape
    row = pl.BlockSpec((1, 1, D), lambda b, i: (b, 0, 0))
    consts = [nw, w_in_p, conv_w, conv_b, bgate, dtb]
    return pl.pallas_call(
        functools.partial(_inproj_kernel, tm=tm),
        grid=(bsz, seq // tm),
        in_specs=[pl.BlockSpec((1, tm, D), lambda b, i: (b, i, 0)), row, row]
        + [_const_spec(a.shape) for a in consts],
        out_specs=[pl.BlockSpec((1, tm, W_ACT), lambda b, i: (b, i, 0)),
                   pl.BlockSpec((1, tm, HP), lambda b, i: (b, i, 0))],
        out_shape=[jax.ShapeDtypeStruct((bsz, seq, W_ACT), BF16),
                   jax.ShapeDtypeStruct((bsz, seq, HP), F32)],
        scratch_shapes=[
            pltpu.VMEM((COL_CHUNK // LANES, SUBLANES + tm, LANES), F32),
            pltpu.VMEM((COL_CHUNK // LANES, SUBLANES + tm, LANES), F32),
            pltpu.VMEM((COL_CHUNK // LANES, tm, LANES), F32),
            pltpu.VMEM((SUBLANES, N_CONV), F32)],
        compiler_params=pltpu.CompilerParams(
            dimension_semantics=("arbitrary", "arbitrary"), vmem_limit_bytes=VMEM_LIMIT),
        name="in_proj",
    )(x, sc, sh, *consts)


def _mixer_ffn_kernel(x_ref, lx_ref, lg_ref, z_ref, xbc_ref, gt_ref, dt_ref,
                      g1_ref, sc2_ref, sh2_ref, g2_ref,
                      post1_ref, wg_ref, ba_ref, bx_ref, lam_ref, wpa_ref,
                      alog_ref, dexp_ref, nw_ref, wpb_ref, wout_ref,
                      pre2_ref, post2_ref, w1_ref, w2_ref,
                      o_ref,
                      pre_scr, ya_scr, hc_scr, yn_scr, st_scr, x1_scr, act_scr,
                      *, ts, nt):
    s = pl.program_id(0)
    cur = s % 2

    @pl.when(s == 0)
    def _():
        x1_scr[...] = jnp.zeros_like(x1_scr)

    @pl.when(s % nt == 0)
    def _():
        hc_scr[...] = jnp.zeros_like(hc_scr)
        st_scr[...] = jnp.zeros_like(st_scr)

    def ffn_pre():
        xp = x1_scr[1 - cur]
        h = (xp * _rms_scale(xp)) * pre2_ref[...]
        ffn["hb"] = (h * (1.0 + sc2_ref[0]) + sh2_ref[0]).astype(BF16)

    def ffn_up(j):
        sl = slice(j * FF_CHUNK, (j + 1) * FF_CHUNK)
        a = jnp.maximum(_dot(ffn["hb"], w1_ref[:, sl]), 0.0)
        act_scr[:, sl] = (a * a).astype(BF16)

    def ffn_down(j):
        sl = slice(j * FF_CHUNK, (j + 1) * FF_CHUNK)
        part = _dot(act_scr[:, sl], w2_ref[sl, :])
        ffn["y"] = part if j == 0 else ffn["y"] + part

    def ffn_post():
        y = ffn["y"]
        o_ref[0] = x1_scr[1 - cur] + g2_ref[0] * ((y * _rms_scale(y)) * post2_ref[...])

    ffn = {}
    ffn_tasks = [ffn_pre]
    for j in range(D_FF // FF_CHUNK):
        ffn_tasks += [functools.partial(ffn_up, j), functools.partial(ffn_down, j)]
    ffn_tasks.append(ffn_post)

    k2 = (-0.5 * LRU_C * LOG2E) * _softplus(-lam_ref[...])
    qw = 4 * LRU_HD
    n_q = LRU_W // qw
    row = lax.broadcasted_iota(jnp.int32, (SUBLANES, qw), 0)
    carries = [hc_scr[:, q * qw:(q + 1) * qw] for q in range(n_q)]

    def lru_gates(q):
        pre_scr[q] = _dot(lx_ref[0, :, q * qw:(q + 1) * qw], wg_ref[q])

    def lru_block(q, rb):
        cs = slice(q * qw, (q + 1) * qw)
        rs = slice(rb * LRU_ROWS, (rb + 1) * LRU_ROWS)
        t_a = jnp.tanh(pre_scr[q, rs, :qw] + ba_ref[:, cs])
        t_x = jnp.tanh(pre_scr[q, rs, qw:] + bx_ref[:, cs])
        a = jnp.exp2(t_a * k2[:, cs] + k2[:, cs])
        xh = 0.5 * lx_ref[0, rs, cs].astype(F32)
        u = jnp.sqrt(1.0 - a * a) * (t_x * xh + xh)
        carry = carries[q]
        hs = []
        for k in range(LRU_ROWS // SUBLANES):
            av = a[k * SUBLANES:(k + 1) * SUBLANES]
            bv = u[k * SUBLANES:(k + 1) * SUBLANES]
            for s in (1, 2, 4):
                keep = row >= s
                a_sh = jnp.where(keep, pltpu.roll(av, s, 0), 1.0)
                b_sh = jnp.where(keep, pltpu.roll(bv, s, 0), 0.0)
                bv = av * b_sh + bv
                av = av * a_sh
            h = bv + av * carry
            hs.append(h)
            carry = jnp.broadcast_to(h[SUBLANES - 1:SUBLANES, :], (SUBLANES, qw))
        carries[q] = carry
        ya_scr[rs, cs] = (jnp.concatenate(hs, axis=0) * lg_ref[0, rs, cs].astype(F32)).astype(BF16)

    lru_tasks = []
    for q in range(n_q):
        lru_tasks.append(functools.partial(lru_gates, q))
        for rb in range(ts // LRU_ROWS):
            lru_tasks.append(functools.partial(lru_block, q, rb))

    dt = dt_ref[0]
    dt_t = dt.T
    d_a_t = (dt * (-jnp.exp(alog_ref[...]))).T

    li = lax.broadcasted_iota(jnp.int32, (SSD_L, SSD_L), 0)
    si = lax.broadcasted_iota(jnp.int32, (SSD_L, SSD_L), 1)
    tri_t = (li <= si).astype(BF16)

    chunk = {}

    def ssd_chunk_setup(c):
        cr = slice(c * SSD_L, (c + 1) * SSD_L)
        csum_t = sum(_dot(p, tri_t) for p in _split_bf16(d_a_t[:, cr], 3))
        last = csum_t[:, SSD_L - 1:SSD_L]
        csum2_t = csum_t * LOG2E
        dt_c = dt_t[:, cr]
        chunk[c] = dict(
            csum2_t=csum2_t, adj=(csum2_t - jnp.log(dt_c) * LOG2E).T, ecs_t=jnp.exp(csum_t),
            w_t=dt_c * jnp.exp(last - csum_t))

    unit = {}
    state = {}
    causal_t = li <= si

    def ssd_stage_a(c, g):
        cr = slice(c * SSD_L, (c + 1) * SSD_L)
        b_g = xbc_ref[0, cr, SSD_INNER + g * SSD_N:SSD_INNER + (g + 1) * SSD_N]
        c_g = xbc_ref[0, cr, SSD_INNER + SSD_BC + g * SSD_N:SSD_INNER + SSD_BC + (g + 1) * SSD_N]
        cb_t = lax.dot_general(b_g, c_g, (((1,), (1,)), ((), ())), preferred_element_type=F32)
        unit[c, g] = dict(cb_t=cb_t, c_t=c_g.astype(F32).T)

    def ssd_stage_b(c, g):
        k = chunk[c]
        u = unit[c, g]
        cb_t, c_t = u.pop("cb_t"), u.pop("c_t")
        rhs = []
        for e in range(SSD_HPG):
            hh = g * SSD_HPG + e
            seg2 = k["csum2_t"][hh:hh + 1, :] - k["adj"][:, hh:hh + 1]
            m_t = (cb_t * jnp.exp2(jnp.where(causal_t, seg2, -jnp.inf))).astype(BF16)
            c2_t = (c_t * k["ecs_t"][hh:hh + 1, :]).astype(BF16)
            rhs.append(jnp.concatenate([m_t, c2_t], axis=0))
        u["rhs"] = rhs

    def ssd_stage_c(c, g):
        cr = slice(c * SSD_L, (c + 1) * SSD_L)
        k = chunk[c]
        rhs = unit.pop((c, g))["rhs"]
        gs = slice(g * SSD_GW, (g + 1) * SSD_GW)
        xs_g = xbc_ref[0, cr, gs].astype(F32)
        x_t = xs_g.T
        s_t = state[g] if g in state else st_scr[g]
        ys_t, xd_t, sdec = [], [], []
        for e in range(SSD_HPG):
            hh = g * SSD_HPG + e
            rows = slice(e * SSD_P, (e + 1) * SSD_P)
            lhs = jnp.concatenate([x_t[rows].astype(BF16), s_t[rows].astype(BF16)], axis=1)
            ys_t.append(_dot(lhs, rhs[e]))
            xd_t.append((x_t[rows] * k["w_t"][hh:hh + 1, :]).astype(BF16))
            sdec.append(s_t[rows] * k["ecs_t"][hh:hh + 1, SSD_L - 1:SSD_L])
        b_g = xbc_ref[0, cr, SSD_INNER + g * SSD_N:SSD_INNER + (g + 1) * SSD_N]
        state[g] = jnp.concatenate(sdec, axis=0) + _dot(jnp.concatenate(xd_t, axis=0), b_g)
        if c == ts // SSD_L - 1:
            st_scr[g] = state[g]
        yg = jnp.concatenate(ys_t, axis=0).T + dexp_ref[:, gs] * xs_g
        yg = yg * z_ref[0, cr, gs].astype(F32)
        yn_scr[cr, gs] = ((yg * _rms_scale(yg)) * nw_ref[:, gs]).astype(BF16)

    for c in range(ts // SSD_L):
        ssd_chunk_setup(c)
    units = [(c, g) for c in range(ts // SSD_L) for g in range(SSD_G)]
    ssd_tasks = []
    for n in range(len(units) + 2):
        for stage, lag in ((ssd_stage_a, 0), (ssd_stage_b, 1), (ssd_stage_c, 2)):
            if 0 <= n - lag < len(units):
                ssd_tasks.append(functools.partial(stage, *units[n - lag]))

    def lru_out():
        for q in range(n_q):
            hc_scr[:, q * qw:(q + 1) * qw] = carries[q]
        ffn["y_a"] = _dot(ya_scr[...], wpa_ref[...])

    lru_tasks.append(lru_out)

    lists = [(lru_tasks, 0.0, 0.6), (ssd_tasks, 0.0, 1.0), (ffn_tasks, 0.0, 1.0)]
    order = sorted((lo + (hi - lo) * (i + 0.5) / len(t), n, i)
                   for n, (t, lo, hi) in enumerate(lists) for i in range(len(t)))
    for _, n, i in order:
        lists[n][0][i]()

    y_a = ffn["y_a"]
    y_b = _dot(yn_scr[...], wpb_ref[...])
    merged = gt_ref[0, :, :D].astype(F32) * y_a + gt_ref[0, :, D:].astype(F32) * y_b
    y = _dot(merged.astype(BF16), wout_ref[...])
    x1_scr[cur] = x_ref[0] + g1_ref[0] * ((y * _rms_scale(y)) * post1_ref[...])


def _mixer_ffn(x, act, dt, g1, sc2, sh2, g2, post1, wg, ba, bx, lam, wpa, alog, dexp, nw, wpb, wout,
               pre2, post2, w1, w2, ts):
    bsz, seq, _ = x.shape
    nt = seq // ts
    n_tiles = bsz * nt

    def tile(width, col_off):
        def index(s):
            m = jnp.minimum(s, n_tiles - 1)
            return (m // nt, m % nt, col_off // width)
        return pl.BlockSpec((1, ts, width), index)

    def mixer_row(s):
        return (jnp.minimum(s, n_tiles - 1) // nt, 0, 0)

    def ffn_row(s):
        return (jnp.maximum(s - 1, 0) // nt, 0, 0)

    def ffn_tile(s):
        f = jnp.maximum(s - 1, 0)
        return (f // nt, f % nt, 0)

    in_specs = [
        tile(D, 0),
        tile(LRU_W, OFF_LX),
        tile(LRU_W, OFF_LG),
        tile(SSD_INNER, OFF_Z),
        tile(SSD_CONV, OFF_XBC),
        tile(2 * D, OFF_GT),
        tile(HP, 0),
        pl.BlockSpec((1, 1, D), mixer_row),
        pl.BlockSpec((1, 1, D), ffn_row),
        pl.BlockSpec((1, 1, D), ffn_row),
        pl.BlockSpec((1, 1, D), ffn_row),
    ]
    consts = [post1, wg, ba, bx, lam, wpa, alog, dexp, nw, wpb, wout, pre2, post2, w1, w2]
    in_specs += [pl.BlockSpec(a.shape, functools.partial(lambda nd, s: (0,) * nd, a.ndim),
                              pipeline_mode=pl.Buffered(1)) for a in consts]
    scratch = [
        pltpu.VMEM((LRU_W // (4 * LRU_HD), ts, 8 * LRU_HD), F32),
        pltpu.VMEM((ts, LRU_W), BF16),
        pltpu.VMEM((SUBLANES, LRU_W), F32),
        pltpu.VMEM((ts, SSD_INNER), BF16),
        pltpu.VMEM((SSD_G, SSD_GW, SSD_N), F32),
        pltpu.VMEM((2, ts, D), F32),
        pltpu.VMEM((ts, D_FF), BF16),
    ]
    return pl.pallas_call(
        functools.partial(_mixer_ffn_kernel, ts=ts, nt=nt),
        grid=(n_tiles + 1,),
        in_specs=in_specs,
        out_specs=pl.BlockSpec((1, ts, D), ffn_tile),
        out_shape=jax.ShapeDtypeStruct((bsz, seq, D), F32),
        scratch_shapes=scratch,
        compiler_params=pltpu.CompilerParams(
            dimension_semantics=("arbitrary",), vmem_limit_bytes=VMEM_LIMIT),
        name="mixer_ffn",
    )(x, act, act, act, act, act, dt, g1, sc2, sh2, g2, *consts)


def _block_diag4(w):
    w4 = w.reshape(LRU_HEADS // 4, 4, LRU_HD, LRU_HD)
    eye = jnp.eye(4, dtype=w.dtype)
    return jnp.einsum('qeij,ef->qeifj', w4, eye).reshape(LRU_HEADS // 4, 4 * LRU_HD, 4 * LRU_HD)


def _pad_heads(v, fill=0.0):
    return jnp.pad(v.reshape(1, SSD_H), ((0, 0), (0, HP - SSD_H)), constant_values=fill)


def _permute_w_in(w):
    o_xbc = 2 * LRU_W + SSD_INNER
    o_dt = o_xbc + SSD_CONV
    o_gt = o_dt + SSD_H
    w = w.astype(BF16)
    return jnp.concatenate(
        [w[:, o_xbc:o_dt], w[:, :o_xbc], w[:, o_gt:], w[:, o_dt:o_gt], jnp.zeros((D, HP - SSD_H), BF16)], axis=1)


def kernel(x, c, w_ada, b_ada, pre_norm1, post_norm1, w_in, b_gate, lru_conv_w, lru_conv_b,
           lru_wa, lru_ba, lru_wx, lru_bx, lru_lambda, w_pa, ssd_conv_w, ssd_conv_b,
           ssd_dt_bias, ssd_a_log, ssd_d, ssd_norm_w, w_pb, w_out, pre_norm2, post_norm2,
           w_ff1, w_ff2):
    depth = w_ada.shape[0]
    bsz, seq, _ = x.shape
    ts = 256 if seq % 256 == 0 else SSD_L
    tm = 512 if seq % 512 == 0 else ts
    for l in range(depth):
        mod = _adaln(c, w_ada[l], b_ada[l]).reshape(bsz, N_MOD, 1, D)
        sh1, sc1, g1, sh2, sc2, g2 = (mod[:, k] for k in range(N_MOD))
        conv_w = jnp.concatenate([ssd_conv_w[l], lru_conv_w[l]], axis=1)
        conv_b = jnp.concatenate([ssd_conv_b[l], lru_conv_b[l]]).reshape(1, N_CONV)
        act, dt = _in_proj(x, sc1, sh1, pre_norm1[l].reshape(1, D), _permute_w_in(w_in[l]),
                           conv_w, conv_b, b_gate[l].reshape(1, 2 * D), _pad_heads(ssd_dt_bias[l]), tm)
        wg = (0.5 * jnp.concatenate([_block_diag4(lru_wa[l]), _block_diag4(lru_wx[l])], axis=2)).astype(BF16)
        x = _mixer_ffn(
            x, act, dt, g1, sc2, sh2, g2, post_norm1[l].reshape(1, D), wg,
            0.5 * lru_ba[l].reshape(1, LRU_W), 0.5 * lru_bx[l].reshape(1, LRU_W),
            lru_lambda[l].reshape(1, LRU_W),
            w_pa[l].astype(BF16), _pad_heads(ssd_a_log[l]),
            jnp.repeat(ssd_d[l], SSD_P).reshape(1, SSD_INNER), ssd_norm_w[l].reshape(1, SSD_INNER),
            w_pb[l].astype(BF16), w_out[l].astype(BF16),
            pre_norm2[l].reshape(1, D), post_norm2[l].reshape(1, D),
            w_ff1[l].astype(BF16), w_ff2[l].astype(BF16), ts)
    return x
```
